```python
import jax, jax.numpy as jnp
from jax import lax
import numpy as np

D_MODEL = 1024
BATCH = 4
SEQ = 4096
DEPTH = 4
DEC_BATCH = 32
DEC_SEQ = 8
PAST_LEN = 8192
PAGE_SIZE = 128

N_MIXERS = 3
N_A_LAYERS = (DEPTH + 2) // 3
N_B_LAYERS = (DEPTH + 1) // 3
N_C_LAYERS = DEPTH // 3
A_HEAD = 64
A_HEADS = D_MODEL // A_HEAD
A_DECAY_LORA = 64
A_ICLR_LORA = 64
A_GATE_LORA = 128
A_GN_EPS = 64e-5
A_N_MIX = 6
B_HEAD = 64
B_HEADS = D_MODEL // B_HEAD
Q_BLOCK = 128
C_WIDTH = 3
D_FF = 2816
FFN_WIDTH = 3
RMS_EPS = 1e-6
POOL_NUM = 5
POOL_DEN = 4

kernel_name = "hybrid_rwkv7_fox_shortconv_convffn_step"


def rmsnorm(x, g):
    xf = x.astype(jnp.float32)
    y = xf * lax.rsqrt(jnp.mean(xf * xf, axis=-1, keepdims=True) + RMS_EPS)
    return (y * g.astype(jnp.float32)).astype(x.dtype)


def causal_dwconv(u, buf, w):
    K = w.shape[0]
    T = u.shape[1]
    full = jnp.concatenate([buf.astype(u.dtype), u], axis=1)
    y = full[:, 0:T] * w[0]
    for j in range(1, K):
        y = y + full[:, j:j + T] * w[j]
    return y, full[:, full.shape[1] - (K - 1):]


def _wkv_step(S, inp):
    r, k, v, w, kk, b = inp
    sa = jnp.einsum('bhij,bhj->bhi', S, -kk)
    S = S * w[:, :, None, :] + sa[..., None] * b[:, :, None, :] + v[..., None] * k[:, :, None, :]
    return S, jnp.einsum('bhij,bhj->bhi', S, r)


def rwkv7_mix(xn, shift, S0, mu, w_rkv, w0, w1, w2, a0, a1, a2, g1, g2, k_k, k_a, r_k, lnx_g, lnx_b, w_out):
    f32 = jnp.float32
    Bn, T, D = xn.shape
    x_prev = jnp.concatenate([shift[:, None].astype(xn.dtype), xn[:, :-1]], axis=1)
    dx = x_prev - xn
    mix = lambda c: xn + dx * mu[c]
    r = mix(0) @ w_rkv[0]
    k = mix(2) @ w_rkv[1]
    v = mix(3) @ w_rkv[2]
    w_log = -jax.nn.softplus(-(w0 + jnp.tanh(mix(1) @ w1) @ w2).astype(f32)) - 0.5
    decay = jnp.exp(-jnp.exp(w_log))
    iclr = jax.nn.sigmoid((a0 + (mix(4) @ a1) @ a2).astype(f32))
    g = jax.nn.sigmoid(mix(5) @ g1) @ g2
    heads = lambda z: z.reshape(Bn, T, A_HEADS, A_HEAD)
    kk = heads((k * k_k).astype(f32))
    kk = kk / jnp.maximum(jnp.linalg.norm(kk, axis=-1, keepdims=True), 1e-12)
    k_h = heads(k.astype(f32) * (1.0 + (iclr - 1.0) * k_a.astype(f32)))
    r_h = heads(r.astype(f32))
    v_h = heads(v.astype(f32))
    b_h = kk * heads(iclr)
    xs = tuple(jnp.moveaxis(z, 1, 0) for z in (r_h, k_h, v_h, heads(decay), kk, b_h))
    S_T, o = lax.scan(_wkv_step, S0.astype(f32), xs)
    o = jnp.moveaxis(o, 0, 1)
    mean = jnp.mean(o, axis=-1, keepdims=True)
    var = jnp.mean(jnp.square(o - mean), axis=-1, keepdims=True)
    o = ((o - mean) * lax.rsqrt(var + A_GN_EPS)).reshape(Bn, T, D) * lnx_g.astype(f32) + lnx_b.astype(f32)
    bonus = jnp.sum(r_h * k_h * r_k.astype(f32), axis=-1, keepdims=True) * v_h
    o = (o + bonus.reshape(Bn, T, D)).astype(xn.dtype) * g
    return o @ w_out, xn[:, -1], S_T.astype(S0.dtype)


def fox_project(xn, w_in, f_bias, q_g, k_g):
    Bn, T, D = xn.shape
    proj = xn @ w_in
    q = rmsnorm(proj[..., 0:D].reshape(Bn, T, B_HEADS, B_HEAD), q_g)
    k = rmsnorm(proj[..., D:2 * D].reshape(Bn, T, B_HEADS, B_HEAD), k_g)
    v = proj[..., 2 * D:3 * D].reshape(Bn, T, B_HEADS, B_HEAD)
    g = proj[..., 3 * D:4 * D]
    logf = jax.nn.log_sigmoid(proj[..., 4 * D:].astype(jnp.float32) + f_bias.astype(jnp.float32))
    return q, k, v, g, logf


def fox_attend(q, k, v, Fq, Fk, q_pos, k_pos):
    Bn, Tq, H, HD = q.shape
    QB = min(Q_BLOCK, Tq)
    nb = Tq // QB
    qb = q.reshape(Bn, nb, QB, H, HD).transpose(1, 0, 2, 3, 4)
    Fqb = Fq.reshape(Bn, nb, QB, H).transpose(1, 0, 3, 2)
    pb = q_pos.reshape(nb, QB)
    Fk_t = Fk.transpose(0, 2, 1)[:, :, None, :]
    scale = HD ** -0.5

    def block(args):
        qi, Fi, pi = args
        s = jnp.einsum('bqhd,bkhd->bhqk', qi, k).astype(jnp.float32) * scale
        s = s + Fi[..., None] - Fk_t
        mask = k_pos[None, :] <= pi[:, None]
        s = jnp.where(mask, s, -jnp.inf)
        p = jax.nn.softmax(s, axis=-1).astype(v.dtype)
        return jnp.einsum('bhqk,bkhd->bqhd', p, v)

    o = lax.map(block, (qb, Fqb, pb))
    return o.transpose(1, 0, 2, 3, 4).reshape(Bn, Tq, H * HD)


def conv_mix(xn, buf, w_in, conv_w, w_out):
    D = xn.shape[-1]
    proj = xn @ w_in
    gb, gc, h = proj[..., 0:D], proj[..., D:2 * D], proj[..., 2 * D:3 * D]
    z, new_buf = causal_dwconv(gc * h, buf, conv_w)
    return (gb * z) @ w_out, new_buf


def conv_ffn(xn, buf, w_up, conv_w, conv_b, w_down):
    u = xn @ w_up
    gate, val = u[..., 0:D_FF], u[..., D_FF:]
    gate_c, new_buf = causal_dwconv(gate, buf, conv_w)
    h = jax.nn.silu(gate_c + conv_b) * val
    return h @ w_down, new_buf


def setup_inputs(seed: int = 0) -> dict:
    key = jax.random.key(seed)
    ks = iter(jax.random.split(key, 64))
    nrm = lambda shape, s=1.0: jax.random.normal(next(ks), shape, jnp.float32) * s
    n_pages = PAST_LEN // PAGE_SIZE
    n_pool = DEC_BATCH * n_pages * POOL_NUM // POOL_DEN
    D = D_MODEL
    inp = {}
    inp['x_prompt'] = nrm((BATCH, SEQ, D))
    inp['x_sample'] = nrm((DEC_BATCH, DEC_SEQ, D))
    inp['state_wkv'] = nrm((N_A_LAYERS, DEC_BATCH, A_HEADS, A_HEAD, A_HEAD), 0.5)
    inp['state_shift'] = nrm((N_A_LAYERS, DEC_BATCH, D))
    inp['cache_k'] = nrm((N_B_LAYERS, n_pool, PAGE_SIZE, B_HEADS, B_HEAD))
    inp['cache_v'] = nrm((N_B_LAYERS, n_pool, PAGE_SIZE, B_HEADS, B_HEAD))
    inp['cache_logf'] = jax.nn.log_sigmoid(3.0 + nrm((N_B_LAYERS, n_pool, PAGE_SIZE, B_HEADS)))
    inp['state_conv_mix'] = nrm((N_C_LAYERS, DEC_BATCH, C_WIDTH - 1, D))
    inp['state_conv_ffn'] = nrm((DEPTH, DEC_BATCH, FFN_WIDTH - 1, D_FF))
    perm = jax.random.permutation(next(ks), n_pool)[:DEC_BATCH * n_pages]
    inp['page_table'] = perm.reshape(DEC_BATCH, n_pages).astype(jnp.int32)
    inp['norm_mix_g'] = 1.0 + nrm((DEPTH, D), 0.01)
    inp['norm_ffn_g'] = 1.0 + nrm((DEPTH, D), 0.01)
    NA = N_A_LAYERS
    inp['a_mu'] = jax.random.uniform(next(ks), (NA, A_N_MIX, D), jnp.float32)
    inp['a_w_rkv'] = nrm((NA, 3, D, D), D ** -0.5)
    inp['a_w0'] = -1.0 + nrm((NA, D), 0.3)
    inp['a_w1'] = nrm((NA, D, A_DECAY_LORA), D ** -0.5)
    inp['a_w2'] = nrm((NA, A_DECAY_LORA, D), A_DECAY_LORA ** -0.5)
    inp['a_a0'] = nrm((NA, D), 0.1)
    inp['a_a1'] = nrm((NA, D, A_ICLR_LORA), D ** -0.5)
    inp['a_a2'] = nrm((NA, A_ICLR_LORA, D), A_ICLR_LORA ** -0.5)
    inp['a_g1'] = nrm((NA, D, A_GATE_LORA), D ** -0.5)
    inp['a_g2'] = nrm((NA, A_GATE_LORA, D), A_GATE_LORA ** -0.5)
    inp['a_k_k'] = 1.0 + nrm((NA, D), 0.1)
    inp['a_k_a'] = 1.0 + nrm((NA, D), 0.1)
    inp['a_r_k'] = nrm((NA, A_HEADS, A_HEAD), 0.1)
    inp['a_lnx_g'] = 1.0 + nrm((NA, D), 0.01)
    inp['a_lnx_b'] = nrm((NA, D), 0.01)
    inp['a_w_out'] = nrm((NA, D, D), D ** -0.5)
    NB = N_B_LAYERS
    inp['b_w_in'] = nrm((NB, D, 4 * D + B_HEADS), D ** -0.5)
    inp['b_f_bias'] = jnp.linspace(1.0, 5.0, B_HEADS, dtype=jnp.float32)[None] + nrm((NB, B_HEADS), 0.1)
    inp['b_q_norm_g'] = 1.0 + nrm((NB, B_HEAD), 0.01)
    inp['b_k_norm_g'] = 1.0 + nrm((NB, B_HEAD), 0.01)
    inp['b_w_out'] = nrm((NB, D, D), D ** -0.5)
    NC = N_C_LAYERS
    inp['c_w_in'] = nrm((NC, D, 3 * D), D ** -0.5)
    inp['c_conv_w'] = nrm((NC, C_WIDTH, D), C_WIDTH ** -0.5)
    inp['c_w_out'] = nrm((NC, D, D), D ** -0.5)
    inp['f_w_up'] = nrm((DEPTH, D, 2 * D_FF), D ** -0.5)
    inp['f_conv_w'] = nrm((DEPTH, FFN_WIDTH, D_FF), FFN_WIDTH ** -0.5)
    inp['f_conv_b'] = nrm((DEPTH, D_FF), 0.01)
    inp['f_w_down'] = nrm((DEPTH, D_FF, D), D_FF ** -0.5)
    return inp


def reference(x_prompt, x_sample, state_wkv, state_shift, cache_k, cache_v, cache_logf, state_conv_mix, state_conv_ffn, page_table,
              norm_mix_g, norm_ffn_g,
              a_mu, a_w_rkv, a_w0, a_w1, a_w2, a_a0, a_a1, a_a2, a_g1, a_g2, a_k_k, a_k_a, a_r_k, a_lnx_g, a_lnx_b, a_w_out,
              b_w_in, b_f_bias, b_q_norm_g, b_k_norm_g, b_w_out,
              c_w_in, c_conv_w, c_w_out,
              f_w_up, f_conv_w, f_conv_b, f_w_down):
    BP, TP, D = x_prompt.shape
    DB, TS, _ = x_sample.shape
    P = page_table.shape[1] * cache_k.shape[2]
    xp, xs = x_prompt, x_sample
    wkv_p, shift_p, wkv_s, shift_s = [], [], [], []
    k_p, v_p, lf_p, k_s, v_s, lf_s = [], [], [], [], [], []
    cm_p, cm_s, cf_p, cf_s = [], [], [], []
    for i in range(DEPTH):
        kind, j = i % N_MIXERS, i // N_MIXERS
        hp = rmsnorm(xp, norm_mix_g[i])
        hs = rmsnorm(xs, norm_mix_g[i])
        if kind == 0:
            pa = (a_mu[j], a_w_rkv[j], a_w0[j], a_w1[j], a_w2[j], a_a0[j], a_a1[j], a_a2[j], a_g1[j], a_g2[j],
                  a_k_k[j], a_k_a[j], a_r_k[j], a_lnx_g[j], a_lnx_b[j], a_w_out[j])
            zero_shift = jnp.zeros((BP, D), xp.dtype)
            zero_S = jnp.zeros((BP, A_HEADS, A_HEAD, A_HEAD), jnp.float32)
            yp, sh1, S1 = rwkv7_mix(hp, zero_shift, zero_S, *pa)
            ys, sh2, S2 = rwkv7_mix(hs, state_shift[j], state_wkv[j], *pa)
            wkv_p.append(S1); shift_p.append(sh1); wkv_s.append(S2); shift_s.append(sh2)
        elif kind == 1:
            pb = (b_w_in[j], b_f_bias[j], b_q_norm_g[j], b_k_norm_g[j])
            q, k, v, g, logf = fox_project(hp, *pb)
            F = jnp.cumsum(logf, axis=1)
            pos = jnp.arange(TP, dtype=jnp.int32)
            o = fox_attend(q, k, v, F, F, pos, pos)
            yp = (o * jax.nn.sigmoid(g)) @ b_w_out[j]
            k_p.append(k); v_p.append(v); lf_p.append(logf.astype(xp.dtype))
            q2, k2, v2, g2, logf2 = fox_project(hs, *pb)
            k_past = cache_k[j][page_table].reshape(DB, P, B_HEADS, B_HEAD).astype(k2.dtype)
            v_past = cache_v[j][page_table].reshape(DB, P, B_HEADS, B_HEAD).astype(v2.dtype)
            lf_past = cache_logf[j][page_table].reshape(DB, P, B_HEADS).astype(jnp.float32)
            F2 = jnp.cumsum(jnp.concatenate([lf_past, logf2], axis=1), axis=1)
            k_all = jnp.concatenate([k_past, k2], axis=1)
            v_all = jnp.concatenate([v_past, v2], axis=1)
            q_pos = P + jnp.arange(TS, dtype=jnp.int32)
            k_pos = jnp.arange(P + TS, dtype=jnp.int32)
            o2 = fox_attend(q2, k_all, v_all, F2[:, P:], F2, q_pos, k_pos)
            ys = (o2 * jax.nn.sigmoid(g2)) @ b_w_out[j]
            k_s.append(k2); v_s.append(v2); lf_s.append(logf2.astype(xs.dtype))
        else:
            zero_buf = jnp.zeros((BP, C_WIDTH - 1, D), xp.dtype)
            yp, b1 = conv_mix(hp, zero_buf, c_w_in[j], c_conv_w[j], c_w_out[j])
            ys, b2 = conv_mix(hs, state_conv_mix[j], c_w_in[j], c_conv_w[j], c_w_out[j])
            cm_p.append(b1); cm_s.append(b2)
        xp = xp + yp
        xs = xs + ys
        zero_fbuf = jnp.zeros((BP, FFN_WIDTH - 1, D_FF), xp.dtype)
        fp, fb1 = conv_ffn(rmsnorm(xp, norm_ffn_g[i]), zero_fbuf, f_w_up[i], f_conv_w[i], f_conv_b[i], f_w_down[i])
        fs, fb2 = conv_ffn(rmsnorm(xs, norm_ffn_g[i]), state_conv_ffn[i], f_w_up[i], f_conv_w[i], f_conv_b[i], f_w_down[i])
        xp = xp + fp
        xs = xs + fs
        cf_p.append(fb1); cf_s.append(fb2)
    return (xp, xs,
            jnp.stack(wkv_p), jnp.stack(shift_p), jnp.stack(wkv_s), jnp.stack(shift_s),
            jnp.stack(k_p), jnp.stack(v_p), jnp.stack(lf_p), jnp.stack(k_s), jnp.stack(v_s), jnp.stack(lf_s),
            jnp.stack(cm_p), jnp.stack(cm_s), jnp.stack(cf_p), jnp.stack(cf_s))
```

```python
import functools

import jax
import jax.numpy as jnp
from jax import lax
from jax.experimental import pallas as pl
from jax.experimental.pallas import tpu as pltpu

HEAD = 64
LANES = 128
SUBLANES = 8
RMS_EPS = 1e-6
GN_EPS = 64e-5
CONV_TAPS = 3
WKV_CHUNK = 64
VMEM_LIMIT = 56 * 1024 * 1024

F32 = jnp.float32
BF16 = jnp.bfloat16
_HI = lax.Precision.HIGHEST


def _mm(x, y, precision=None):
    return jnp.dot(x, y, precision=precision, preferred_element_type=F32)


def _nt(x, y, precision=None):
    return lax.dot_general(x, y, (((1,), (1,)), ((), ())), precision=precision,
                           preferred_element_type=F32)


def _tn(x, y, precision=None):
    return lax.dot_general(x, y, (((0,), (0,)), ((), ())), precision=precision,
                           preferred_element_type=F32)


def _iota(shape, dim):
    return lax.broadcasted_iota(jnp.int32, shape, dim)


def _sigmoid(x):
    return 1.0 / (1.0 + jnp.exp(-x))


def _softplus(x):
    return jnp.maximum(x, 0.0) + jnp.log(1.0 + jnp.exp(-jnp.abs(x)))


def _rms(x, g):
    return x * lax.rsqrt(jnp.mean(x * x, axis=-1, keepdims=True) + RMS_EPS) * g


def _pair_ones():
    r = lax.div(_iota((LANES, LANES), 0), HEAD)
    c = lax.div(_iota((LANES, LANES), 1), HEAD)
    return (r == c).astype(F32)


def _head_sums(x, ones):
    parts = [_mm(x[:, i:i + LANES], ones, _HI) for i in range(0, x.shape[1], LANES)]
    return jnp.concatenate(parts, axis=1)


def _stage_rows(sh_ref, cols, u, first_tile):
    tm = u.shape[0]

    @pl.when(first_tile)
    def _():
        sh_ref[0:SUBLANES, cols] = jnp.zeros((SUBLANES, u.shape[1]), u.dtype)

    sh_ref[SUBLANES:SUBLANES + tm, cols] = u


def _prev_rows(sh_ref, cols, tm, k, pos, fix_ref):
    prev = sh_ref[SUBLANES - k:SUBLANES - k + tm, cols]
    if fix_ref is not None:
        prev = jnp.where(pos >= k, prev, fix_ref[:, cols])
    return prev


def _keep_tail(sh_ref, tm):
    sh_ref[0:SUBLANES, :] = sh_ref[tm:tm + SUBLANES, :]


def _seq_pos(tm, seq_rows):
    return lax.rem(_iota((tm, 1), 0), seq_rows)


def _rwkv_proj_kernel(*refs, tiles_per_seq, seq_rows):
    short = seq_rows < refs[0].shape[0]
    x_ref, refs = refs[0], refs[1:]
    fix_ref = None
    if short:
        fix_ref, refs = refs[0], refs[1:]
    (ng_ref, mu_ref, w0_ref, a0_ref, kk_ref, ka_ref,
     wr_ref, wk_ref, wv_ref, w1_ref, w2_ref, a1_ref, a2_ref, g1_ref, g2_ref,
     r_o, k_o, v_o, lw_o, kkn_o, b_o, g_o, xn_o, sh_ref) = refs
    tm = x_ref.shape[0]
    i = pl.program_id(0)
    xn = _rms(x_ref[...], ng_ref[...])
    xn_o[...] = xn
    cols = slice(None)
    _stage_rows(sh_ref, cols, xn, lax.rem(i, tiles_per_seq) == 0)
    pos = _seq_pos(tm, seq_rows) if short else None
    dx = _prev_rows(sh_ref, cols, tm, 1, pos, fix_ref) - xn
    _keep_tail(sh_ref, tm)

    def mix(c):
        return (xn + dx * mu_ref[c:c + 1, :]).astype(BF16)

    r = _mm(mix(0), wr_ref[...])
    k = _mm(mix(2), wk_ref[...])
    v = _mm(mix(3), wv_ref[...])
    wl = w0_ref[...] + _mm(jnp.tanh(_mm(mix(1), w1_ref[...])).astype(BF16), w2_ref[...])
    lw = -jnp.exp(-_softplus(-wl) - 0.5)
    iclr = _sigmoid(a0_ref[...] + _mm(_mm(mix(4), a1_ref[...]).astype(BF16), a2_ref[...]))
    g = _mm(_sigmoid(_mm(mix(5), g1_ref[...])).astype(BF16), g2_ref[...])
    kk = k * kk_ref[...]
    norm = jnp.sqrt(_head_sums(kk * kk, _pair_ones()))
    kk = kk / jnp.maximum(norm, 1e-12)
    r_o[...] = r
    k_o[...] = k * (1.0 + (iclr - 1.0) * ka_ref[...])
    v_o[...] = v
    lw_o[...] = lw
    kkn_o[...] = kk
    b_o[...] = kk * iclr
    g_o[...] = g


def _wkv_kernel(r_ref, k_ref, v_ref, lw_ref, kk_ref, b_ref, s0_ref, o_ref, so_ref, s_ref):
    c = pl.program_id(1)
    chunk, d = r_ref.shape
    n_pairs = d // LANES

    @pl.when(c == 0)
    def _():
        s_ref[...] = s0_ref[...]

    lw = lw_ref[...]
    tril = (_iota((chunk, chunk), 0) >= _iota((chunk, chunk), 1)).astype(F32)
    cum = _mm(tril, lw, _HI)
    e_pos = jnp.exp(cum)
    e_neg = jnp.exp(-cum)
    a_t = -(kk_ref[...] * jnp.exp(cum - lw))
    b_t = b_ref[...] * e_neg
    k_t = k_ref[...] * e_neg
    r_t = r_ref[...] * e_pos
    v_t = v_ref[...]
    g_end = e_pos[chunk - 1:chunk, :]

    first = _iota((chunk, LANES), 1) < HEAD
    rows2 = 2 * chunk
    ri = lax.rem(_iota((rows2, rows2), 0), chunk)
    ci = lax.rem(_iota((rows2, rows2), 1), chunk)
    strict = ri > ci
    incl = ri >= ci
    eye = (_iota((rows2, rows2), 0) == _iota((rows2, rows2), 1)).astype(F32)

    def stack(x):
        return jnp.concatenate([jnp.where(first, x, 0.0), jnp.where(first, 0.0, x)], axis=0)

    doublings = max(chunk.bit_length() - 2, 0)
    for p in range(n_pairs):
        sl = slice(p * LANES, (p + 1) * LANES)
        ax, bx, kx, rx, vx = (stack(t[:, sl]) for t in (a_t, b_t, k_t, r_t, v_t))
        l_ab = jnp.where(strict, _nt(ax, bx, _HI), 0.0)
        l_ak = jnp.where(strict, _nt(ax, kx, _HI), 0.0)
        m_rb = jnp.where(incl, _nt(rx, bx, _HI), 0.0)
        m_rk = jnp.where(incl, _nt(rx, kx, _HI), 0.0)
        inv = eye + l_ab
        pw = l_ab
        for _ in range(doublings):
            pw = _mm(pw, pw, _HI)
            inv = inv + _mm(pw, inv, _HI)
        s = s_ref[p]
        u = _mm(inv, _nt(ax, s, _HI) + _mm(l_ak, vx, _HI), _HI)
        oe = _nt(rx, s, _HI) + _mm(m_rb, u, _HI) + _mm(m_rk, vx, _HI)
        o_ref[:, sl] = oe[:chunk] + oe[chunk:]
        s_ref[p] = (s + _tn(u, bx, _HI) + _tn(vx, kx, _HI)) * g_end[:, sl]

    @pl.when(c == pl.num_programs(1) - 1)
    def _():
        so_ref[...] = s_ref[...]


def _rwkv_out_kernel(o_ref, r_ref, k_ref, v_ref, g_ref, x_ref, lng_ref, lnb_ref, rk_ref,
                     wo_ref, y_ref):
    ones = _pair_ones()
    o = o_ref[...]
    d = o - _head_sums(o, ones) * (1.0 / HEAD)
    var = _head_sums(d * d, ones) * (1.0 / HEAD)
    on = d * lax.rsqrt(var + GN_EPS) * lng_ref[...] + lnb_ref[...]
    bonus = _head_sums(r_ref[...] * k_ref[...] * rk_ref[...], ones) * v_ref[...]
    z = ((on + bonus) * g_ref[...]).astype(BF16)
    y_ref[...] = x_ref[...] + _mm(z, wo_ref[...])


def _fox_proj_kernel(x_ref, ng_ref, wq_ref, wk_ref, wv_ref, wg_ref, wf_ref, fb_ref, qg_ref,
                     kg_ref, q_o, k_o, v_o, g_o, lf_o, fc_o, carry_ref, *, tiles_per_seq):
    tm = x_ref.shape[0]
    i = pl.program_id(0)
    xn = _rms(x_ref[...], ng_ref[...]).astype(BF16)
    ones = _pair_ones()
    q = _mm(xn, wq_ref[...])
    q_o[...] = q * lax.rsqrt(_head_sums(q * q, ones) * (1.0 / HEAD) + RMS_EPS) * qg_ref[...]
    k = _mm(xn, wk_ref[...])
    k_o[...] = k * lax.rsqrt(_head_sums(k * k, ones) * (1.0 / HEAD) + RMS_EPS) * kg_ref[...]
    v_o[...] = _mm(xn, wv_ref[...])
    g_o[...] = _mm(xn, wg_ref[...])
    lf = -_softplus(-(_mm(xn, wf_ref[...]) + fb_ref[...]))
    lf_o[...] = lf

    @pl.when(lax.rem(i, tiles_per_seq) == 0)
    def _():
        carry_ref[...] = jnp.zeros(carry_ref.shape, F32)

    tril = (_iota((tm, tm), 0) >= _iota((tm, tm), 1)).astype(F32)
    fc = _mm(tril, lf, _HI) + carry_ref[0:1, :]
    fc_o[...] = fc
    carry_ref[...] = jnp.broadcast_to(fc[tm - 1:tm, :], carry_ref.shape)


def _fox_attn_kernel(q_ref, k_ref, v_ref, fq_ref, fk_ref, o_ref):
    qi = pl.program_id(2)
    tq = q_ref.shape[0]
    tk = k_ref.shape[1]
    q = q_ref[...] * (HEAD ** -0.5)
    first = _iota((tq, LANES), 1) < HEAD
    qh = (jnp.where(first, q, 0.0).astype(BF16), jnp.where(first, 0.0, q).astype(BF16))
    fq = (fq_ref[:, 0:1], fq_ref[:, 1:2])
    row = _iota((tq, tk), 0) + qi * tq
    col = _iota((tq, tk), 1)

    def body(j, carry):
        kb = k_ref[j].astype(BF16)
        vb = v_ref[j].astype(BF16)
        fk = fk_ref[j]
        keep = col + j * tk <= row
        out = []
        for h in range(2):
            m, l, acc = carry[3 * h:3 * h + 3]
            s = _nt(qh[h], kb) + fq[h] - fk[h:h + 1, :]
            s = jnp.where(keep, s, -jnp.inf)
            m_new = jnp.maximum(m, jnp.max(s, axis=1, keepdims=True))
            p = jnp.exp(s - m_new)
            alpha = jnp.exp(m - m_new)
            l = alpha * l + jnp.sum(p, axis=1, keepdims=True)
            acc = alpha * acc + _mm(p.astype(BF16), vb)
            out += [m_new, l, acc]
        return tuple(out)

    init = (jnp.full((tq, 1), -jnp.inf, F32), jnp.zeros((tq, 1), F32),
            jnp.zeros((tq, LANES), F32)) * 2
    m0, l0, acc0, m1, l1, acc1 = lax.fori_loop(0, qi + 1, body, init)
    o_ref[...] = jnp.where(first, acc0 / l0, acc1 / l1)


def _fox_paged_kernel(pt_ref, q_ref, k_ref, v_ref, lf_ref, ck_ref, cv_ref, clf_ref, o_ref,
                      qx_ref, m_ref, l_ref, acc_ref, ct_ref, tot_ref):
    del pt_ref
    s_idx = pl.program_id(1)
    ts, d = q_ref.shape
    n_heads = d // HEAD
    rows = n_heads * ts

    def online(sc, vb):
        m = m_ref[...]
        m_new = jnp.maximum(m, jnp.max(sc, axis=1, keepdims=True))
        p = jnp.exp(sc - m_new)
        alpha = jnp.exp(m - m_new)
        l_ref[...] = alpha * l_ref[...] + jnp.sum(p, axis=1, keepdims=True)
        acc_ref[...] = alpha * acc_ref[...] + _mm(p.astype(BF16), vb)
        m_ref[...] = m_new

    @pl.when(s_idx == 0)
    def _():
        q = q_ref[...] * (HEAD ** -0.5)
        qt = jnp.concatenate([q] * n_heads, axis=0)
        own = lax.div(_iota((rows, d), 0), ts) == lax.div(_iota((rows, d), 1), HEAD)
        qx = jnp.where(own, qt, 0.0).astype(BF16)
        qx_ref[...] = qx
        lf = lf_ref[...]
        hsel = (lax.div(_iota((rows, lf.shape[1]), 0), ts)
                == _iota((rows, lf.shape[1]), 1)).astype(F32)
        lfx = _nt(hsel, lf, _HI)
        upto = (_iota((ts, ts), 0) <= _iota((ts, ts), 1)).astype(F32)
        cinc = _mm(lfx, upto, _HI)
        tq = lax.rem(_iota((rows, ts), 0), ts)
        sk = _iota((rows, ts), 1)
        ct = jnp.sum(jnp.where(tq == sk, cinc, 0.0), axis=1, keepdims=True)
        ct_ref[...] = ct
        tot_ref[...] = jnp.zeros(tot_ref.shape, F32)
        m_ref[...] = jnp.full(m_ref.shape, -jnp.inf, F32)
        l_ref[...] = jnp.zeros(l_ref.shape, F32)
        acc_ref[...] = jnp.zeros(acc_ref.shape, F32)
        sc = _nt(qx, k_ref[...].astype(BF16)) + ct - cinc
        online(jnp.where(sk <= tq, sc, -jnp.inf), v_ref[...].astype(BF16))

    @pl.when(s_idx > 0)
    def _():
        lfp = clf_ref[...]
        ps = lfp.shape[0]
        hsel = (lax.div(_iota((rows, n_heads), 0), ts) == _iota((rows, n_heads), 1)).astype(F32)
        lfx = _nt(hsel, lfp, _HI)
        after = (_iota((ps, ps), 0) > _iota((ps, ps), 1)).astype(F32)
        tot = tot_ref[...]
        sc = (_nt(qx_ref[...], ck_ref[...].astype(BF16)) + ct_ref[...] + tot
              + _mm(lfx, after, _HI))
        online(sc, cv_ref[...].astype(BF16))
        tot_ref[...] = tot + jnp.sum(lfx, axis=1, keepdims=True)

    @pl.when(s_idx == pl.num_programs(1) - 1)
    def _():
        own = lax.div(_iota((rows, d), 0), ts) == lax.div(_iota((rows, d), 1), HEAD)
        full = jnp.where(own, acc_ref[...] / l_ref[...], 0.0)
        o_ref[...] = jnp.sum(full.reshape(n_heads, ts, d), axis=0)


def _fox_out_kernel(o_ref, g_ref, x_ref, wo_ref, y_ref):
    z = (o_ref[...] * _sigmoid(g_ref[...])).astype(BF16)
    y_ref[...] = x_ref[...] + _mm(z, wo_ref[...])


def _conv_mix_kernel(*refs, tiles_per_seq, seq_rows):
    short = seq_rows < refs[0].shape[0]
    x_ref, refs = refs[0], refs[1:]
    f1_ref = f2_ref = None
    if short:
        f1_ref, f2_ref, refs = refs[0], refs[1], refs[2:]
    ng_ref, win_ref, cw_ref, wo_ref, y_ref, tail_ref, sh_ref = refs
    tm, d = x_ref.shape
    i = pl.program_id(0)
    x = x_ref[...]
    xn = _rms(x, ng_ref[...]).astype(BF16)
    gb = _mm(xn, win_ref[:, 0:d])
    u = _mm(xn, win_ref[:, d:2 * d]) * _mm(xn, win_ref[:, 2 * d:3 * d])
    cols = slice(None)
    _stage_rows(sh_ref, cols, u, lax.rem(i, tiles_per_seq) == 0)
    pos = _seq_pos(tm, seq_rows) if short else None
    z = (cw_ref[0:1, :] * _prev_rows(sh_ref, cols, tm, 2, pos, f2_ref)
         + cw_ref[1:2, :] * _prev_rows(sh_ref, cols, tm, 1, pos, f1_ref)
         + cw_ref[2:3, :] * u)
    y_ref[...] = x + _mm((gb * z).astype(BF16), wo_ref[...])
    tr = tail_ref.shape[0]
    tail_ref[...] = sh_ref[SUBLANES + tm - tr:SUBLANES + tm, :]
    _keep_tail(sh_ref, tm)


def _ffn_kernel(*refs, tiles_per_seq, seq_rows, n_chunks):
    short = seq_rows < refs[0].shape[0]
    x_ref, refs = refs[0], refs[1:]
    f1_ref = f2_ref = None
    if short:
        f1_ref, f2_ref, refs = refs[0], refs[1], refs[2:]
    ng_ref, wup_ref, cw_ref, cb_ref, wdn_ref, y_ref, tail_ref, sh_ref = refs
    tm = x_ref.shape[0]
    dff = cw_ref.shape[1]
    w = dff // n_chunks
    i = pl.program_id(0)
    x = x_ref[...]
    xn = _rms(x, ng_ref[...]).astype(BF16)
    pos = _seq_pos(tm, seq_rows) if short else None
    first_tile = lax.rem(i, tiles_per_seq) == 0
    acc = x
    for j in range(n_chunks):
        cols = slice(j * w, (j + 1) * w)
        gate = _mm(xn, wup_ref[:, j * w:(j + 1) * w])
        val = _mm(xn, wup_ref[:, dff + j * w:dff + (j + 1) * w])
        _stage_rows(sh_ref, cols, gate, first_tile)
        conv = (cw_ref[0:1, cols] * _prev_rows(sh_ref, cols, tm, 2, pos, f2_ref)
                + cw_ref[1:2, cols] * _prev_rows(sh_ref, cols, tm, 1, pos, f1_ref)
                + cw_ref[2:3, cols] * gate + cb_ref[:, cols])
        h = conv * _sigmoid(conv) * val
        acc = acc + _mm(h.astype(BF16), wdn_ref[j * w:(j + 1) * w, :])
    y_ref[...] = acc
    tr = tail_ref.shape[0]
    tail_ref[...] = sh_ref[SUBLANES + tm - tr:SUBLANES + tm, :]
    _keep_tail(sh_ref, tm)


class _Group:
    def __init__(self, n_seq, seq_rows, max_tile):
        self.n_seq, self.seq_rows = n_seq, seq_rows
        self.rows = n_seq * seq_rows
        if seq_rows % max_tile == 0:
            self.tm = max_tile
        else:
            self.tm = self.rows
            assert seq_rows % SUBLANES == 0 and seq_rows >= CONV_TAPS - 1
        self.short = seq_rows < self.tm
        self.tiles_per_seq = max(seq_rows // self.tm, 1)
        self.n_tiles = self.rows // self.tm
        self.tail_rows = self.tm if self.short else SUBLANES

    def row_spec(self, c):
        return pl.BlockSpec((self.tm, c), lambda i: (i, 0))

    def tail_spec(self, c):
        return pl.BlockSpec((self.tail_rows, c), lambda i: (i, 0))

    def last_rows(self, tail, n):
        c = tail.shape[-1]
        if self.short:
            return tail.reshape(self.n_seq, self.seq_rows, c)[:, self.seq_rows - n:]
        t = tail.reshape(self.n_seq, self.tiles_per_seq, SUBLANES, c)
        return t[:, -1, SUBLANES - n:]

    def fix_rows(self, state, k):
        n_prev, c = state.shape[1], state.shape[2]
        pad = jnp.zeros((self.n_seq, self.seq_rows - k, c), state.dtype)
        return jnp.concatenate([state[:, n_prev - k:], pad], axis=1).reshape(self.rows, c)


def _whole(a):
    nd = a.ndim
    return pl.BlockSpec(a.shape, lambda *_: (0,) * nd)


def _params(*sem):
    return pltpu.CompilerParams(dimension_semantics=sem, vmem_limit_bytes=VMEM_LIMIT)


def _row2(v):
    return v.reshape(1, -1).astype(F32)


def _rwkv_proj(grp, x, shift, ng, mu, w0, a0, k_k, k_a, weights):
    d = x.shape[1]
    vecs = [_row2(ng), mu.astype(F32), _row2(w0), _row2(a0), _row2(k_k), _row2(k_a)]
    ins, specs = [x], [grp.row_spec(d)]
    if grp.short:
        ins.append(grp.fix_rows(shift[:, None, :], 1))
        specs.append(grp.row_spec(d))
    consts = vecs + list(weights)
    out = jax.ShapeDtypeStruct((grp.rows, d), F32)
    return pl.pallas_call(
        functools.partial(_rwkv_proj_kernel, tiles_per_seq=grp.tiles_per_seq,
                          seq_rows=grp.seq_rows),
        grid=(grp.n_tiles,),
        in_specs=specs + [_whole(a) for a in consts],
        out_specs=[grp.row_spec(d)] * 8,
        out_shape=[out] * 8,
        scratch_shapes=[pltpu.VMEM((grp.tm + 2 * SUBLANES, d), F32)],
        compiler_params=_params("arbitrary"),
        name="rwkv_proj",
    )(*ins, *consts)


def _wkv(n_seq, seq_rows, chunk, r, k, v, lw, kk, b, s0):
    d = r.shape[1]
    n_pairs = d // LANES
    n_chunks = seq_rows // chunk
    tok = pl.BlockSpec((chunk, d), lambda bi, c: (bi * n_chunks + c, 0))
    st = pl.BlockSpec((None, n_pairs, LANES, LANES), lambda bi, c: (bi, 0, 0, 0))
    return pl.pallas_call(
        _wkv_kernel,
        grid=(n_seq, n_chunks),
        in_specs=[tok] * 6 + [st],
        out_specs=[tok, st],
        out_shape=[jax.ShapeDtypeStruct(r.shape, F32), jax.ShapeDtypeStruct(s0.shape, F32)],
        scratch_shapes=[pltpu.VMEM((n_pairs, LANES, LANES), F32)],
        compiler_params=_params("arbitrary", "arbitrary"),
        name="wkv",
    )(r, k, v, lw, kk, b, s0)


def _rwkv_out(grp, o, r, k, v, g, x, lng, lnb, rk, wo):
    d = x.shape[1]
    consts = [_row2(lng), _row2(lnb), _row2(rk), wo]
    return pl.pallas_call(
        _rwkv_out_kernel,
        grid=(grp.n_tiles,),
        in_specs=[grp.row_spec(d)] * 6 + [_whole(a) for a in consts],
        out_specs=grp.row_spec(d),
        out_shape=jax.ShapeDtypeStruct(x.shape, F32),
        compiler_params=_params("arbitrary"),
        name="rwkv_out",
    )(o, r, k, v, g, x, *consts)


def _pair_state(s):
    n, h = s.shape[0], s.shape[1]
    s = s.reshape(n, h // 2, 2, HEAD, HEAD).astype(F32)
    z = jnp.zeros_like(s[:, :, 0])
    top = jnp.concatenate([s[:, :, 0], z], axis=-1)
    bot = jnp.concatenate([z, s[:, :, 1]], axis=-1)
    return jnp.concatenate([top, bot], axis=-2)


def _unpair_state(s, dtype):
    n, p = s.shape[0], s.shape[1]
    s = s.reshape(n, p, 2, HEAD, 2, HEAD)
    out = jnp.stack([s[:, :, 0, :, 0, :], s[:, :, 1, :, 1, :]], axis=2)
    return out.reshape(n, 2 * p, HEAD, HEAD).astype(dtype)


def _fox_proj(grp, x, ng, wq, wk, wv, wg, wf, fb, qg, kg):
    d = x.shape[1]
    consts = [_row2(ng), wq, wk, wv, wg, wf, fb, qg, kg]
    wide = jax.ShapeDtypeStruct((grp.rows, d), F32)
    thin = jax.ShapeDtypeStruct((grp.rows, LANES), F32)
    return pl.pallas_call(
        functools.partial(_fox_proj_kernel, tiles_per_seq=grp.tiles_per_seq),
        grid=(grp.n_tiles,),
        in_specs=[grp.row_spec(d)] + [_whole(a) for a in consts],
        out_specs=[grp.row_spec(d)] * 4 + [grp.row_spec(LANES)] * 2,
        out_shape=[wide] * 4 + [thin] * 2,
        scratch_shapes=[pltpu.VMEM((SUBLANES, LANES), F32)],
        compiler_params=_params("arbitrary"),
        name="fox_proj",
    )(x, *consts)


def _fox_attn(n_seq, seq_rows, tq, q, k, v, fcum):
    d = q.shape[1]
    n_pairs = d // LANES
    nq = seq_rows // tq
    h = d // HEAD
    f = fcum[:, :h].reshape(n_seq, seq_rows, n_pairs, 2)
    fcol = f.transpose(0, 2, 1, 3)
    frow = f.transpose(0, 2, 3, 1).reshape(n_seq, n_pairs, 2, nq, tq).transpose(0, 1, 3, 2, 4)
    kv = pl.BlockSpec((None, nq, tq, LANES), lambda b, p, i: (b, 0, 0, p))
    qo = pl.BlockSpec((tq, LANES), lambda b, p, i: (b * nq + i, p))
    return pl.pallas_call(
        _fox_attn_kernel,
        grid=(n_seq, n_pairs, nq),
        in_specs=[qo, kv, kv,
                  pl.BlockSpec((None, None, tq, 2), lambda b, p, i: (b, p, i, 0)),
                  pl.BlockSpec((None, None, nq, 2, tq), lambda b, p, i: (b, p, 0, 0, 0))],
        out_specs=qo,
        out_shape=jax.ShapeDtypeStruct(q.shape, F32),
        compiler_params=_params("arbitrary", "arbitrary", "arbitrary"),
        name="fox_attn",
    )(q, k.reshape(n_seq, nq, tq, d), v.reshape(n_seq, nq, tq, d), fcol, frow)


def _fox_paged(n_seq, ts, q, k, v, lf, cache_k, cache_v, cache_lf, page_table):
    d = q.shape[1]
    n_heads = d // HEAD
    rows = n_heads * ts
    n_pool, ps = cache_k.shape[0], cache_k.shape[1]
    n_pages = page_table.shape[1]
    ck = cache_k.reshape(n_pool, ps, d)
    cv = cache_v.reshape(n_pool, ps, d)

    def page(b, s, pt):
        return pt[b * n_pages + n_pages - jnp.maximum(s, 1)]

    tok = lambda c: pl.BlockSpec((ts, c), lambda b, s, pt: (b, 0))
    grid_spec = pltpu.PrefetchScalarGridSpec(
        num_scalar_prefetch=1,
        grid=(n_seq, n_pages + 1),
        in_specs=[tok(d), tok(d), tok(d), tok(LANES),
                  pl.BlockSpec((None, ps, d), lambda b, s, pt: (page(b, s, pt), 0, 0)),
                  pl.BlockSpec((None, ps, d), lambda b, s, pt: (page(b, s, pt), 0, 0)),
                  pl.BlockSpec((None, ps, n_heads), lambda b, s, pt: (page(b, s, pt), 0, 0))],
        out_specs=tok(d),
        scratch_shapes=[pltpu.VMEM((rows, d), BF16), pltpu.VMEM((rows, 1), F32),
                        pltpu.VMEM((rows, 1), F32), pltpu.VMEM((rows, d), F32),
                        pltpu.VMEM((rows, 1), F32), pltpu.VMEM((rows, 1), F32)],
    )
    return pl.pallas_call(
        _fox_paged_kernel,
        grid_spec=grid_spec,
        out_shape=jax.ShapeDtypeStruct(q.shape, F32),
        compiler_params=_params("arbitrary", "arbitrary"),
        name="fox_paged",
    )(page_table.reshape(-1).astype(jnp.int32), q, k, v, lf, ck, cv, cache_lf.astype(F32))


def _fox_out(grp, o, g, x, wo):
    d = x.shape[1]
    return pl.pallas_call(
        _fox_out_kernel,
        grid=(grp.n_tiles,),
        in_specs=[grp.row_spec(d)] * 3 + [_whole(wo)],
        out_specs=grp.row_spec(d),
        out_shape=jax.ShapeDtypeStruct(x.shape, F32),
        compiler_params=_params("arbitrary"),
        name="fox_out",
    )(o, g, x, wo)


def _conv_mix(grp, x, state, ng, win, cw, wo):
    d = x.shape[1]
    ins, specs = [x], [grp.row_spec(d)]
    if grp.short:
        ins += [grp.fix_rows(state, 1), grp.fix_rows(state, 2)]
        specs += [grp.row_spec(d)] * 2
    consts = [_row2(ng), win, cw.astype(F32), wo]
    return pl.pallas_call(
        functools.partial(_conv_mix_kernel, tiles_per_seq=grp.tiles_per_seq,
                          seq_rows=grp.seq_rows),
        grid=(grp.n_tiles,),
        in_specs=specs + [_whole(a) for a in consts],
        out_specs=[grp.row_spec(d), grp.tail_spec(d)],
        out_shape=[jax.ShapeDtypeStruct(x.shape, F32),
                   jax.ShapeDtypeStruct((grp.n_tiles * grp.tail_rows, d), F32)],
        scratch_shapes=[pltpu.VMEM((grp.tm + 2 * SUBLANES, d), F32)],
        compiler_params=_params("arbitrary"),
        name="conv_mix",
    )(*ins, *consts)


def _ffn(grp, x, state, ng, wup, cw, cb, wdn):
    d = x.shape[1]
    dff = cw.shape[1]
    n_chunks = 2 if dff % (2 * LANES) == 0 else 1
    ins, specs = [x], [grp.row_spec(d)]
    if grp.short:
        ins += [grp.fix_rows(state, 1), grp.fix_rows(state, 2)]
        specs += [grp.row_spec(dff)] * 2
    consts = [_row2(ng), wup, cw.astype(F32), _row2(cb), wdn]
    return pl.pallas_call(
        functools.partial(_ffn_kernel, tiles_per_seq=grp.tiles_per_seq, seq_rows=grp.seq_rows,
                          n_chunks=n_chunks),
        grid=(grp.n_tiles,),
        in_specs=specs + [_whole(a) for a in consts],
        out_specs=[grp.row_spec(d), grp.tail_spec(dff)],
        out_shape=[jax.ShapeDtypeStruct(x.shape, F32),
                   jax.ShapeDtypeStruct((grp.n_tiles * grp.tail_rows, dff), F32)],
        scratch_shapes=[pltpu.VMEM((grp.tm + 2 * SUBLANES, dff), F32)],
        compiler_params=_params("arbitrary"),
        name="conv_ffn",
    )(*ins, *consts)


def kernel(x_prompt, x_sample, state_wkv, state_shift, cache_k, cache_v, cache_logf, state_conv_mix, state_conv_ffn, page_table, norm_mix_g, norm_ffn_g, a_mu, a_w_rkv, a_w0, a_w1, a_w2, a_a0, a_a1, a_a2, a_g1, a_g2, a_k_k, a_k_a, a_r_k, a_lnx_g, a_lnx_b, a_w_out, b_w_in, b_f_bias, b_q_norm_g, b_k_norm_g, b_w_out, c_w_in, c_conv_w, c_w_out, f_w_up, f_conv_w, f_conv_b, f_w_down):
    bp, tp, d = x_prompt.shape
    db, ts, _ = x_sample.shape
    depth = norm_mix_g.shape[0]
    n_heads = d // HEAD
    assert d % LANES == 0 and c_conv_w.shape[1] == CONV_TAPS and f_conv_w.shape[1] == CONV_TAPS
    gp = _Group(bp, tp, 256)
    gs = _Group(db, ts, 256)
    groups = (gp, gs)
    xs = [x_prompt.reshape(gp.rows, d), x_sample.reshape(gs.rows, d)]
    bf = lambda w: w.astype(BF16)
    chunk_p = WKV_CHUNK if tp % WKV_CHUNK == 0 else tp
    assert chunk_p <= WKV_CHUNK

    wkv_o, shift_o = ([], []), ([], [])
    k_o, v_o, lf_o = ([], []), ([], []), ([], [])
    cm_o, cf_o = ([], []), ([], [])
    for i in range(depth):
        kind, j = i % 3, i // 3
        if kind == 0:
            weights = [bf(a_w_rkv[j, 0]), bf(a_w_rkv[j, 1]), bf(a_w_rkv[j, 2]), bf(a_w1[j]),
                       bf(a_w2[j]), bf(a_a1[j]), bf(a_a2[j]), bf(a_g1[j]), bf(a_g2[j])]
            wo = bf(a_w_out[j])
            for gi, grp in enumerate(groups):
                shift = state_shift[j] if gi == 1 else None
                r, k, v, lw, kk, b, g, xn = _rwkv_proj(
                    grp, xs[gi], shift, norm_mix_g[i], a_mu[j], a_w0[j], a_a0[j], a_k_k[j],
                    a_k_a[j], weights)
                if gi == 0:
                    s0 = jnp.zeros((grp.n_seq, d // LANES, LANES, LANES), F32)
                    chunk = chunk_p
                else:
                    s0 = _pair_state(state_wkv[j])
                    chunk = ts
                o, s_new = _wkv(grp.n_seq, grp.seq_rows, chunk, r, k, v, lw, kk, b, s0)
                xs[gi] = _rwkv_out(grp, o, r, k, v, g, xs[gi], a_lnx_g[j], a_lnx_b[j],
                                   a_r_k[j].reshape(-1), wo)
                wkv_o[gi].append(_unpair_state(s_new, state_wkv.dtype if gi else F32))
                shift_o[gi].append(xn.reshape(grp.n_seq, grp.seq_rows, d)[:, -1])
        elif kind == 1:
            w_in = b_w_in[j]
            wq, wk, wv, wg = (bf(w_in[:, c * d:(c + 1) * d]) for c in range(4))
            wf = bf(jnp.pad(w_in[:, 4 * d:], ((0, 0), (0, LANES - n_heads))))
            fb = jnp.pad(b_f_bias[j].astype(F32), (0, LANES - n_heads)).reshape(1, LANES)
            qg = jnp.tile(b_q_norm_g[j].astype(F32), n_heads).reshape(1, d)
            kg = jnp.tile(b_k_norm_g[j].astype(F32), n_heads).reshape(1, d)
            wo = bf(b_w_out[j])
            for gi, grp in enumerate(groups):
                q, k, v, g, lf, fc = _fox_proj(grp, xs[gi], norm_mix_g[i], wq, wk, wv, wg, wf,
                                               fb, qg, kg)
                if gi == 0:
                    o = _fox_attn(grp.n_seq, grp.seq_rows, grp.tm, q, k, v, fc)
                else:
                    o = _fox_paged(grp.n_seq, grp.seq_rows, q, k, v, lf, cache_k[j], cache_v[j],
                                   cache_logf[j], page_table)
                xs[gi] = _fox_out(grp, o, g, xs[gi], wo)
                shape = (grp.n_seq, grp.seq_rows, n_heads, HEAD)
                k_o[gi].append(k.reshape(shape))
                v_o[gi].append(v.reshape(shape))
                lf_o[gi].append(lf[:, :n_heads].reshape(shape[:3]))
        else:
            win, wo = bf(c_w_in[j]), bf(c_w_out[j])
            for gi, grp in enumerate(groups):
                state = state_conv_mix[j] if gi == 1 else None
                xs[gi], tail = _conv_mix(grp, xs[gi], state, norm_mix_g[i], win, c_conv_w[j], wo)
                cm_o[gi].append(grp.last_rows(tail, CONV_TAPS - 1))
        wup, wdn = bf(f_w_up[i]), bf(f_w_down[i])
        for gi, grp in enumerate(groups):
            state = state_conv_ffn[i] if gi == 1 else None
            xs[gi], tail = _ffn(grp, xs[gi], state, norm_ffn_g[i], wup, f_conv_w[i],
                                f_conv_b[i], wdn)
            cf_o[gi].append(grp.last_rows(tail, CONV_TAPS - 1))

    st = jnp.stack
    return (xs[0].reshape(bp, tp, d), xs[1].reshape(db, ts, d),
            st(wkv_o[0]), st(shift_o[0]), st(wkv_o[1]), st(shift_o[1]),
            st(k_o[0]), st(v_o[0]), st(lf_o[0]), st(k_o[1]), st(v_o[1]), st(lf_o[1]),
            st(cm_o[0]), st(cm_o[1]), st(cf_o[0]), st(cf_o[1]))
```

```python
import functools

import jax
import jax.numpy as jnp
from jax import lax
from jax.experimental import pallas as pl
from jax.experimental.pallas import tpu as pltpu

HEAD = 64
LANES = 128
SUBLANES = 8
RMS_EPS = 1e-6
GN_EPS = 64e-5
CONV_TAPS = 3
WKV_CHUNK = 64
ROW_TILE = 256
ATTN_TILE = 512
VMEM_LIMIT = 56 * 1024 * 1024

F32 = jnp.float32
BF16 = jnp.bfloat16
_HI = lax.Precision.HIGHEST


def _mm(x, y, precision=None):
    return jnp.dot(x, y, precision=precision, preferred_element_type=F32)


def _nt(x, y, precision=None):
    return lax.dot_general(x, y, (((1,), (1,)), ((), ())), precision=precision,
                           preferred_element_type=F32)


def _tn(x, y, precision=None):
    return lax.dot_general(x, y, (((0,), (0,)), ((), ())), precision=precision,
                           preferred_element_type=F32)


def _iota(shape, dim):
    return lax.broadcasted_iota(jnp.int32, shape, dim)


def _sigmoid(x):
    return 1.0 / (1.0 + jnp.exp(-x))


def _softplus(x):
    return jnp.maximum(x, 0.0) + jnp.log(1.0 + jnp.exp(-jnp.abs(x)))


def _rms(x, g):
    return x * lax.rsqrt(jnp.mean(x * x, axis=-1, keepdims=True) + RMS_EPS) * g


def _pair_ones():
    r = lax.div(_iota((LANES, LANES), 0), HEAD)
    c = lax.div(_iota((LANES, LANES), 1), HEAD)
    return (r == c).astype(F32)


def _head_sums(x, ones):
    parts = [_mm(x[:, i:i + LANES], ones, _HI) for i in range(0, x.shape[1], LANES)]
    return jnp.concatenate(parts, axis=1)


def _stage_rows(sh_ref, cols, u, first_tile):
    tm = u.shape[0]

    @pl.when(first_tile)
    def _():
        sh_ref[0:SUBLANES, cols] = jnp.zeros((SUBLANES, u.shape[1]), u.dtype)

    sh_ref[SUBLANES:SUBLANES + tm, cols] = u


def _prev_rows(sh_ref, cols, tm, k, pos, fix_ref):
    prev = sh_ref[SUBLANES - k:SUBLANES - k + tm, cols]
    if fix_ref is not None:
        prev = jnp.where(pos >= k, prev, fix_ref[:, cols])
    return prev


def _keep_tail(sh_ref, tm):
    sh_ref[0:SUBLANES, :] = sh_ref[tm:tm + SUBLANES, :]


def _seq_pos(tm, seq_rows):
    return lax.rem(_iota((tm, 1), 0), seq_rows)


def _rwkv_proj_kernel(*refs, tiles_per_seq, seq_rows):
    short = seq_rows < refs[0].shape[0]
    x_ref, refs = refs[0], refs[1:]
    fix_ref = None
    if short:
        fix_ref, refs = refs[0], refs[1:]
    (ng_ref, mu_ref, w0_ref, a0_ref, kk_ref, ka_ref,
     wr_ref, wk_ref, wv_ref, w1_ref, w2_ref, a1_ref, a2_ref, g1_ref, g2_ref,
     r_o, k_o, v_o, lw_o, kkn_o, b_o, g_o, xn_o, sh_ref) = refs
    tm = x_ref.shape[0]
    i = pl.program_id(0)
    xn = _rms(x_ref[...], ng_ref[...])
    xn_o[...] = xn
    cols = slice(None)
    _stage_rows(sh_ref, cols, xn, lax.rem(i, tiles_per_seq) == 0)
    pos = _seq_pos(tm, seq_rows) if short else None
    dx = _prev_rows(sh_ref, cols, tm, 1, pos, fix_ref) - xn
    _keep_tail(sh_ref, tm)

    def mix(c):
        return (xn + dx * mu_ref[c:c + 1, :]).astype(BF16)

    r = _mm(mix(0), wr_ref[...])
    k = _mm(mix(2), wk_ref[...])
    v = _mm(mix(3), wv_ref[...])
    wl = w0_ref[...] + _mm(jnp.tanh(_mm(mix(1), w1_ref[...])).astype(BF16), w2_ref[...])
    lw = -jnp.exp(-_softplus(-wl) - 0.5)
    iclr = _sigmoid(a0_ref[...] + _mm(_mm(mix(4), a1_ref[...]).astype(BF16), a2_ref[...]))
    g = _mm(_sigmoid(_mm(mix(5), g1_ref[...])).astype(BF16), g2_ref[...])
    kk = k * kk_ref[...]
    norm = jnp.sqrt(_head_sums(kk * kk, _pair_ones()))
    kk = kk / jnp.maximum(norm, 1e-12)
    r_o[...] = r
    k_o[...] = k * (1.0 + (iclr - 1.0) * ka_ref[...])
    v_o[...] = v
    lw_o[...] = lw
    kkn_o[...] = kk
    b_o[...] = kk * iclr
    g_o[...] = g


def _wkv_kernel(r_ref, k_ref, v_ref, lw_ref, kk_ref, b_ref, s0_ref, o_ref, so_ref, s_ref):
    c = pl.program_id(1)
    chunk, d = r_ref.shape
    n_pairs = d // LANES

    @pl.when(c == 0)
    def _():
        s_ref[...] = s0_ref[...]

    lw = lw_ref[...]
    tril = (_iota((chunk, chunk), 0) >= _iota((chunk, chunk), 1)).astype(F32)
    cum = _mm(tril, lw, _HI)
    e_pos = jnp.exp(cum)
    e_neg = jnp.exp(-cum)
    a_t = -(kk_ref[...] * jnp.exp(cum - lw))
    b_t = b_ref[...] * e_neg
    k_t = k_ref[...] * e_neg
    r_t = r_ref[...] * e_pos
    v_t = v_ref[...]
    g_end = e_pos[chunk - 1:chunk, :]

    first = _iota((chunk, LANES), 1) < HEAD
    rows2 = 2 * chunk
    merged = rows2 % LANES == 0
    ri = lax.rem(_iota((rows2, rows2), 0), chunk)
    ci = lax.rem(_iota((rows2, rows2), 1), chunk)
    strict = ri > ci
    incl = ri >= ci
    eye = (_iota((rows2, rows2), 0) == _iota((rows2, rows2), 1)).astype(F32)

    def stack(x):
        return jnp.concatenate([jnp.where(first, x, 0.0), jnp.where(first, 0.0, x)],
                               axis=0).astype(BF16)

    def scores(ar, bk):
        if merged:
            g = _nt(ar, bk)
            return g[:rows2, :rows2], g[:rows2, rows2:], g[rows2:, :rows2], g[rows2:, rows2:]
        a, r, b, k = ar[:rows2], ar[rows2:], bk[:rows2], bk[rows2:]
        return _nt(a, b), _nt(a, k), _nt(r, b), _nt(r, k)

    pairs = range(n_pairs)
    lanes = [slice(p * LANES, (p + 1) * LANES) for p in pairs]
    ar = [jnp.concatenate([stack(a_t[:, sl]), stack(r_t[:, sl])], axis=0) for sl in lanes]
    bk = [jnp.concatenate([stack(b_t[:, sl]), stack(k_t[:, sl])], axis=0) for sl in lanes]
    vx = [stack(v_t[:, sl]) for sl in lanes]
    sc = [scores(ar[p], bk[p]) for p in pairs]
    pw = [jnp.where(strict, sc[p][0], 0.0) for p in pairs]
    inv = [eye + pw[p] for p in pairs]
    for _ in range(max(chunk.bit_length() - 2, 0)):
        pwb = [pw[p].astype(BF16) for p in pairs]
        pw = [_mm(pwb[p], pwb[p]) for p in pairs]
        inv = [inv[p] + _mm(pw[p].astype(BF16), inv[p].astype(BF16)) for p in pairs]
    for p in pairs:
        l_ak = jnp.where(strict, sc[p][1], 0.0).astype(BF16)
        m_rb = jnp.where(incl, sc[p][2], 0.0).astype(BF16)
        m_rk = jnp.where(incl, sc[p][3], 0.0).astype(BF16)
        s = s_ref[p]
        xs = _nt(ar[p], s.astype(BF16))
        rhs = xs[:rows2] + _mm(l_ak, vx[p])
        u = _mm(inv[p].astype(BF16), rhs.astype(BF16)).astype(BF16)
        uv = jnp.concatenate([u, vx[p]], axis=0)
        if merged:
            oe = xs[rows2:] + _mm(jnp.concatenate([m_rb, m_rk], axis=1), uv)
        else:
            oe = xs[rows2:] + _mm(m_rb, u) + _mm(m_rk, vx[p])
        o_ref[:, lanes[p]] = oe[:chunk] + oe[chunk:]
        s_ref[p] = (s + _tn(uv, bk[p])) * g_end[:, lanes[p]]

    @pl.when(c == pl.num_programs(1) - 1)
    def _():
        so_ref[...] = s_ref[...]


def _rwkv_out_kernel(o_ref, r_ref, k_ref, v_ref, g_ref, x_ref, lng_ref, lnb_ref, rk_ref,
                     wo_ref, y_ref):
    ones = _pair_ones()
    o = o_ref[...]
    d = o - _head_sums(o, ones) * (1.0 / HEAD)
    var = _head_sums(d * d, ones) * (1.0 / HEAD)
    on = d * lax.rsqrt(var + GN_EPS) * lng_ref[...] + lnb_ref[...]
    bonus = _head_sums(r_ref[...] * k_ref[...] * rk_ref[...], ones) * v_ref[...]
    z = ((on + bonus) * g_ref[...]).astype(BF16)
    y_ref[...] = x_ref[...] + _mm(z, wo_ref[...])


def _fox_proj_kernel(x_ref, ng_ref, wq_ref, wk_ref, wv_ref, wg_ref, wf_ref, fb_ref, qg_ref,
                     kg_ref, q_o, k_o, v_o, g_o, lf_o, fc_o, carry_ref, *, tiles_per_seq):
    tm = x_ref.shape[0]
    i = pl.program_id(0)
    xn = _rms(x_ref[...], ng_ref[...]).astype(BF16)
    ones = _pair_ones()
    q = _mm(xn, wq_ref[...])
    q_o[...] = q * lax.rsqrt(_head_sums(q * q, ones) * (1.0 / HEAD) + RMS_EPS) * qg_ref[...]
    k = _mm(xn, wk_ref[...])
    k_o[...] = k * lax.rsqrt(_head_sums(k * k, ones) * (1.0 / HEAD) + RMS_EPS) * kg_ref[...]
    v_o[...] = _mm(xn, wv_ref[...])
    g_o[...] = _mm(xn, wg_ref[...])
    lf = -_softplus(-(_mm(xn, wf_ref[...]) + fb_ref[...]))
    lf_o[...] = lf

    @pl.when(lax.rem(i, tiles_per_seq) == 0)
    def _():
        carry_ref[...] = jnp.zeros(carry_ref.shape, F32)

    tril = (_iota((tm, tm), 0) >= _iota((tm, tm), 1)).astype(F32)
    fc = _mm(tril, lf, _HI) + carry_ref[0:1, :]
    fc_o[...] = fc
    carry_ref[...] = jnp.broadcast_to(fc[tm - 1:tm, :], carry_ref.shape)


def _fox_attn_kernel(q_ref, k_ref, v_ref, f_ref, o_ref, kaug_ref, vt_ref):
    qi = pl.program_id(2)
    n_blocks, tk, _ = k_ref.shape
    tq = q_ref.shape[0]
    lane = _iota((tk, LANES), 1)

    def augment(x, f, h, piece_off, const_off, const):
        rel = lane - HEAD * (1 - h)
        fh = f[:, h:h + 1]
        hi = fh.astype(BF16).astype(F32)
        mid = (fh - hi).astype(BF16).astype(F32)
        lo = fh - hi - mid
        in_const = jnp.where(rel >= const_off, jnp.where(rel < const_off + 3, const, 0.0), 0.0)
        extra = jnp.where(rel == piece_off, hi,
                          jnp.where(rel == piece_off + 1, mid,
                                    jnp.where(rel == piece_off + 2, lo, in_const)))
        own = lax.div(lane, HEAD) == h
        return jnp.where(own, x, extra).astype(BF16)

    @pl.when(qi == 0)
    def _():
        def fill(j, carry):
            kb = k_ref[j]
            f = f_ref[j]
            for h in range(2):
                kaug_ref[h, j] = augment(kb, f, h, 0, 3, 1.0)
            vt_ref[j] = v_ref[j].T.astype(BF16)
            return carry
        lax.fori_loop(0, n_blocks, fill, 0)

    fq = f_ref[qi]
    q = q_ref[...] * (HEAD ** -0.5)
    qa = [augment(q, fq, h, 3, 0, -1.0) for h in range(2)]
    keep = _iota((tk, tq), 0) <= _iota((tk, tq), 1)

    def block(j, carry, diagonal):
        heads = range(2)
        m, l, acc = carry[0::3], carry[1::3], carry[2::3]
        vt = vt_ref[j]
        s = [_nt(kaug_ref[h, j], qa[h]) for h in heads]
        if diagonal:
            s = [jnp.where(keep, s[h], -jnp.inf) for h in heads]
        m_new = [jnp.maximum(m[h], jnp.max(s[h], axis=0, keepdims=True)) for h in heads]
        p = [jnp.exp(s[h] - m_new[h]) for h in heads]
        alpha = [jnp.exp(m[h] - m_new[h]) for h in heads]
        l = [alpha[h] * l[h] + jnp.sum(p[h], axis=0, keepdims=True) for h in heads]
        acc = [alpha[h] * acc[h] + _mm(vt, p[h].astype(BF16)) for h in heads]
        return (m_new[0], l[0], acc[0], m_new[1], l[1], acc[1])

    init = (jnp.full((1, tq), -jnp.inf, F32), jnp.zeros((1, tq), F32),
            jnp.zeros((LANES, tq), F32)) * 2
    carry = lax.fori_loop(0, qi, lambda j, c: block(j, c, False), init)
    m0, l0, acc0, m1, l1, acc1 = block(qi, carry, True)
    top = _iota((LANES, tq), 0) < HEAD
    o_ref[...] = jnp.where(top, acc0 / l0, acc1 / l1).T


def _fox_suffix_kernel(pt_ref, clf_ref, o_ref, tot_ref):
    del pt_ref

    @pl.when(pl.program_id(1) == 0)
    def _():
        tot_ref[...] = jnp.zeros(tot_ref.shape, F32)

    lf = clf_ref[...]
    ps = lf.shape[0]
    after = (_iota((ps, ps), 1) > _iota((ps, ps), 0)).astype(F32)
    tot = tot_ref[0:1, :]
    o_ref[...] = _mm(after, lf, _HI) + tot
    tot_ref[...] = jnp.broadcast_to(tot + jnp.sum(lf, axis=0, keepdims=True), tot_ref.shape)


def _fox_paged_kernel(pt_ref, q_ref, k_ref, v_ref, lf_ref, ck_ref, cv_ref, suf_ref, o_ref,
                      qc_ref, mask_ref, m_ref, l_ref, acc_ref):
    del pt_ref
    s_idx = pl.program_id(1)
    ts, d = q_ref.shape
    n_heads = d // HEAD
    rows = n_heads * ts

    def online(sc, pv):
        m = m_ref[...]
        m_new = jnp.maximum(m, jnp.max(sc, axis=1, keepdims=True))
        p = jnp.exp(sc - m_new)
        alpha = jnp.exp(m - m_new)
        l_ref[...] = alpha * l_ref[...] + jnp.sum(p, axis=1, keepdims=True)
        acc_ref[...] = alpha * acc_ref[...] + pv(p.astype(BF16))
        m_ref[...] = m_new

    @pl.when(s_idx == 0)
    def _():
        q = q_ref[...] * (HEAD ** -0.5)
        qc_ref[...] = jnp.concatenate(
            [q[:, h * HEAD:(h + 1) * HEAD] for h in range(n_heads)], axis=0).astype(BF16)
        n_lanes = mask_ref.shape[1]
        own = lax.rem(_iota((rows, n_lanes), 1), n_heads) == lax.div(_iota((rows, n_lanes), 0), ts)
        mask_ref[...] = jnp.where(own, 0.0, -jnp.inf)
        m_ref[...] = jnp.full(m_ref.shape, -jnp.inf, F32)
        l_ref[...] = jnp.zeros(l_ref.shape, F32)
        acc_ref[...] = jnp.zeros(acc_ref.shape, F32)
        qt = jnp.concatenate([q] * n_heads, axis=0)
        mine = lax.div(_iota((rows, d), 0), ts) == lax.div(_iota((rows, d), 1), HEAD)
        qx = jnp.where(mine, qt, 0.0).astype(BF16)
        lf = lf_ref[...]
        hsel = (lax.div(_iota((rows, lf.shape[1]), 0), ts)
                == _iota((rows, lf.shape[1]), 1)).astype(F32)
        lfx = _nt(hsel, lf, _HI)
        upto = (_iota((ts, ts), 0) <= _iota((ts, ts), 1)).astype(F32)
        cinc = _mm(lfx, upto, _HI)
        tq = lax.rem(_iota((rows, ts), 0), ts)
        sk = _iota((rows, ts), 1)
        sc = _nt(qx, k_ref[...].astype(BF16)) - cinc
        vb = v_ref[...].astype(BF16)

        def pv(p):
            full = _mm(p, vb)
            return jnp.concatenate(
                [full[h * ts:(h + 1) * ts, h * HEAD:(h + 1) * HEAD] for h in range(n_heads)],
                axis=0)

        online(jnp.where(sk <= tq, sc, -jnp.inf), pv)

    @pl.when(s_idx > 0)
    def _():
        ps = ck_ref.shape[0]
        kf = ck_ref[...].reshape(ps * n_heads, HEAD).astype(BF16)
        vf = cv_ref[...].reshape(ps * n_heads, HEAD).astype(BF16)
        sc = _nt(qc_ref[...], kf) + mask_ref[...] + suf_ref[...]
        online(sc, lambda p: _mm(p, vf))

    @pl.when(s_idx == pl.num_programs(1) - 1)
    def _():
        o = acc_ref[...] / l_ref[...]
        o_ref[...] = jnp.concatenate([o[h * ts:(h + 1) * ts, :] for h in range(n_heads)], axis=1)


def _fox_out_kernel(o_ref, g_ref, x_ref, wo_ref, y_ref):
    z = (o_ref[...] * _sigmoid(g_ref[...])).astype(BF16)
    y_ref[...] = x_ref[...] + _mm(z, wo_ref[...])


def _conv_mix_kernel(*refs, tiles_per_seq, seq_rows):
    short = seq_rows < refs[0].shape[0]
    x_ref, refs = refs[0], refs[1:]
    f1_ref = f2_ref = None
    if short:
        f1_ref, f2_ref, refs = refs[0], refs[1], refs[2:]
    ng_ref, win_ref, cw_ref, wo_ref, y_ref, tail_ref, sh_ref = refs
    tm, d = x_ref.shape
    i = pl.program_id(0)
    x = x_ref[...]
    xn = _rms(x, ng_ref[...]).astype(BF16)
    gb = _mm(xn, win_ref[:, 0:d])
    u = _mm(xn, win_ref[:, d:2 * d]) * _mm(xn, win_ref[:, 2 * d:3 * d])
    cols = slice(None)
    _stage_rows(sh_ref, cols, u, lax.rem(i, tiles_per_seq) == 0)
    pos = _seq_pos(tm, seq_rows) if short else None
    z = (cw_ref[0:1, :] * _prev_rows(sh_ref, cols, tm, 2, pos, f2_ref)
         + cw_ref[1:2, :] * _prev_rows(sh_ref, cols, tm, 1, pos, f1_ref)
         + cw_ref[2:3, :] * u)
    y_ref[...] = x + _mm((gb * z).astype(BF16), wo_ref[...])
    tr = tail_ref.shape[0]
    tail_ref[...] = sh_ref[SUBLANES + tm - tr:SUBLANES + tm, :]
    _keep_tail(sh_ref, tm)


def _ffn_kernel(*refs, tiles_per_seq, seq_rows, n_chunks):
    short = seq_rows < refs[0].shape[0]
    x_ref, refs = refs[0], refs[1:]
    f1_ref = f2_ref = None
    if short:
        f1_ref, f2_ref, refs = refs[0], refs[1], refs[2:]
    ng_ref, wup_ref, cw_ref, cb_ref, wdn_ref, y_ref, tail_ref, sh_ref = refs
    tm = x_ref.shape[0]
    dff = cw_ref.shape[1]
    w = dff // n_chunks
    i = pl.program_id(0)
    x = x_ref[...]
    xn = _rms(x, ng_ref[...]).astype(BF16)
    pos = _seq_pos(tm, seq_rows) if short else None
    first_tile = lax.rem(i, tiles_per_seq) == 0
    acc = x
    for j in range(n_chunks):
        cols = slice(j * w, (j + 1) * w)
        gate = _mm(xn, wup_ref[:, j * w:(j + 1) * w])
        val = _mm(xn, wup_ref[:, dff + j * w:dff + (j + 1) * w])
        _stage_rows(sh_ref, cols, gate, first_tile)
        conv = (cw_ref[0:1, cols] * _prev_rows(sh_ref, cols, tm, 2, pos, f2_ref)
                + cw_ref[1:2, cols] * _prev_rows(sh_ref, cols, tm, 1, pos, f1_ref)
                + cw_ref[2:3, cols] * gate + cb_ref[:, cols])
        h = conv * _sigmoid(conv) * val
        acc = acc + _mm(h.astype(BF16), wdn_ref[j * w:(j + 1) * w, :])
    y_ref[...] = acc
    tr = tail_ref.shape[0]
    tail_ref[...] = sh_ref[SUBLANES + tm - tr:SUBLANES + tm, :]
    _keep_tail(sh_ref, tm)


class _Group:
    def __init__(self, n_seq, seq_rows, max_tile):
        self.n_seq, self.seq_rows = n_seq, seq_rows
        self.rows = n_seq * seq_rows
        if seq_rows % max_tile == 0:
            self.tm = max_tile
        else:
            self.tm = self.rows
            assert seq_rows % SUBLANES == 0 and seq_rows >= CONV_TAPS - 1
        self.short = seq_rows < self.tm
        self.tiles_per_seq = max(seq_rows // self.tm, 1)
        self.n_tiles = self.rows // self.tm
        self.tail_rows = self.tm if self.short else SUBLANES

    def row_spec(self, c):
        return pl.BlockSpec((self.tm, c), lambda i: (i, 0))

    def tail_spec(self, c):
        return pl.BlockSpec((self.tail_rows, c), lambda i: (i, 0))

    def last_rows(self, tail, n):
        c = tail.shape[-1]
        if self.short:
            return tail.reshape(self.n_seq, self.seq_rows, c)[:, self.seq_rows - n:]
        t = tail.reshape(self.n_seq, self.tiles_per_seq, SUBLANES, c)
        return t[:, -1, SUBLANES - n:]

    def fix_rows(self, state, k):
        n_prev, c = state.shape[1], state.shape[2]
        pad = jnp.zeros((self.n_seq, self.seq_rows - k, c), state.dtype)
        return jnp.concatenate([state[:, n_prev - k:], pad], axis=1).reshape(self.rows, c)


def _whole(a):
    nd = a.ndim
    return pl.BlockSpec(a.shape, lambda *_: (0,) * nd)


def _params(*sem):
    return pltpu.CompilerParams(dimension_semantics=sem, vmem_limit_bytes=VMEM_LIMIT)


def _row2(v):
    return v.reshape(1, -1).astype(F32)


def _rwkv_proj(grp, x, shift, ng, mu, w0, a0, k_k, k_a, weights):
    d = x.shape[1]
    vecs = [_row2(ng), mu.astype(F32), _row2(w0), _row2(a0), _row2(k_k), _row2(k_a)]
    ins, specs = [x], [grp.row_spec(d)]
    if grp.short:
        ins.append(grp.fix_rows(shift[:, None, :], 1))
        specs.append(grp.row_spec(d))
    consts = vecs + list(weights)
    out = jax.ShapeDtypeStruct((grp.rows, d), F32)
    return pl.pallas_call(
        functools.partial(_rwkv_proj_kernel, tiles_per_seq=grp.tiles_per_seq,
                          seq_rows=grp.seq_rows),
        grid=(grp.n_tiles,),
        in_specs=specs + [_whole(a) for a in consts],
        out_specs=[grp.row_spec(d)] * 8,
        out_shape=[out] * 8,
        scratch_shapes=[pltpu.VMEM((grp.tm + 2 * SUBLANES, d), F32)],
        compiler_params=_params("arbitrary"),
        name="rwkv_proj",
    )(*ins, *consts)


def _wkv(n_seq, seq_rows, chunk, r, k, v, lw, kk, b, s0):
    d = r.shape[1]
    n_pairs = d // LANES
    n_chunks = seq_rows // chunk
    tok = pl.BlockSpec((chunk, d), lambda bi, c: (bi * n_chunks + c, 0))
    st = pl.BlockSpec((None, n_pairs, LANES, LANES), lambda bi, c: (bi, 0, 0, 0))
    return pl.pallas_call(
        _wkv_kernel,
        grid=(n_seq, n_chunks),
        in_specs=[tok] * 6 + [st],
        out_specs=[tok, st],
        out_shape=[jax.ShapeDtypeStruct(r.shape, F32), jax.ShapeDtypeStruct(s0.shape, F32)],
        scratch_shapes=[pltpu.VMEM((n_pairs, LANES, LANES), F32)],
        compiler_params=_params("arbitrary", "arbitrary"),
        name="wkv",
    )(r, k, v, lw, kk, b, s0)


def _rwkv_out(grp, o, r, k, v, g, x, lng, lnb, rk, wo):
    d = x.shape[1]
    consts = [_row2(lng), _row2(lnb), _row2(rk), wo]
    return pl.pallas_call(
        _rwkv_out_kernel,
        grid=(grp.n_tiles,),
        in_specs=[grp.row_spec(d)] * 6 + [_whole(a) for a in consts],
        out_specs=grp.row_spec(d),
        out_shape=jax.ShapeDtypeStruct(x.shape, F32),
        compiler_params=_params("arbitrary"),
        name="rwkv_out",
    )(o, r, k, v, g, x, *consts)


def _pair_state(s):
    n, h = s.shape[0], s.shape[1]
    s = s.reshape(n, h // 2, 2, HEAD, HEAD).astype(F32)
    z = jnp.zeros_like(s[:, :, 0])
    top = jnp.concatenate([s[:, :, 0], z], axis=-1)
    bot = jnp.concatenate([z, s[:, :, 1]], axis=-1)
    return jnp.concatenate([top, bot], axis=-2)


def _unpair_state(s, dtype):
    n, p = s.shape[0], s.shape[1]
    s = s.reshape(n, p, 2, HEAD, 2, HEAD)
    out = jnp.stack([s[:, :, 0, :, 0, :], s[:, :, 1, :, 1, :]], axis=2)
    return out.reshape(n, 2 * p, HEAD, HEAD).astype(dtype)


def _fox_proj(grp, x, ng, wq, wk, wv, wg, wf, fb, qg, kg):
    d = x.shape[1]
    consts = [_row2(ng), wq, wk, wv, wg, wf, fb, qg, kg]
    wide = jax.ShapeDtypeStruct((grp.rows, d), F32)
    thin = jax.ShapeDtypeStruct((grp.rows, LANES), F32)
    return pl.pallas_call(
        functools.partial(_fox_proj_kernel, tiles_per_seq=grp.tiles_per_seq),
        grid=(grp.n_tiles,),
        in_specs=[grp.row_spec(d)] + [_whole(a) for a in consts],
        out_specs=[grp.row_spec(d)] * 4 + [grp.row_spec(LANES)] * 2,
        out_shape=[wide] * 4 + [thin] * 2,
        scratch_shapes=[pltpu.VMEM((SUBLANES, LANES), F32)],
        compiler_params=_params("arbitrary"),
        name="fox_proj",
    )(x, *consts)


def _fox_attn(n_seq, seq_rows, tq, q, k, v, fcum):
    d = q.shape[1]
    n_pairs = d // LANES
    nq = seq_rows // tq
    h = d // HEAD
    f = fcum[:, :h].reshape(n_seq, nq, tq, n_pairs, 2).transpose(0, 3, 1, 2, 4)
    kv = pl.BlockSpec((None, nq, tq, LANES), lambda b, p, i: (b, 0, 0, p))
    qo = pl.BlockSpec((tq, LANES), lambda b, p, i: (b * nq + i, p))
    return pl.pallas_call(
        _fox_attn_kernel,
        grid=(n_seq, n_pairs, nq),
        in_specs=[qo, kv, kv,
                  pl.BlockSpec((None, None, nq, tq, 2), lambda b, p, i: (b, p, 0, 0, 0))],
        out_specs=qo,
        out_shape=jax.ShapeDtypeStruct(q.shape, F32),
        scratch_shapes=[pltpu.VMEM((2, nq, tq, LANES), BF16), pltpu.VMEM((nq, LANES, tq), BF16)],
        compiler_params=_params("arbitrary", "arbitrary", "arbitrary"),
        name="fox_attn",
    )(q, k.reshape(n_seq, nq, tq, d), v.reshape(n_seq, nq, tq, d), f)


def _fox_paged(n_seq, ts, q, k, v, lf, cache_k, cache_v, cache_lf, page_table):
    d = q.shape[1]
    n_heads = d // HEAD
    rows = n_heads * ts
    ps = cache_k.shape[1]
    n_pages = page_table.shape[1]
    pt = page_table.reshape(-1).astype(jnp.int32)

    suffix = pl.pallas_call(
        _fox_suffix_kernel,
        grid_spec=pltpu.PrefetchScalarGridSpec(
            num_scalar_prefetch=1,
            grid=(n_seq, n_pages),
            in_specs=[pl.BlockSpec((None, ps, n_heads),
                                   lambda b, s, pt: (pt[b * n_pages + n_pages - 1 - s], 0, 0))],
            out_specs=pl.BlockSpec((None, None, ps, n_heads),
                                   lambda b, s, pt: (b, n_pages - 1 - s, 0, 0)),
            scratch_shapes=[pltpu.VMEM((SUBLANES, n_heads), F32)],
        ),
        out_shape=jax.ShapeDtypeStruct((n_seq, n_pages, ps, n_heads), F32),
        compiler_params=_params("arbitrary", "arbitrary"),
        name="fox_suffix",
    )(pt, cache_lf.astype(F32))
    suffix = suffix.reshape(n_seq, n_pages, 1, ps * n_heads)

    def page(b, s, pt):
        return n_pages - jnp.maximum(s, 1)

    def slot(b, s, pt):
        return pt[b * n_pages + page(b, s, pt)]

    tok = lambda c: pl.BlockSpec((ts, c), lambda b, s, pt: (b, 0))
    cache = pl.BlockSpec((None, ps, n_heads, HEAD), lambda b, s, pt: (slot(b, s, pt), 0, 0, 0))
    grid_spec = pltpu.PrefetchScalarGridSpec(
        num_scalar_prefetch=1,
        grid=(n_seq, n_pages + 1),
        in_specs=[tok(d), tok(d), tok(d), tok(LANES), cache, cache,
                  pl.BlockSpec((None, None, 1, ps * n_heads),
                               lambda b, s, pt: (b, page(b, s, pt), 0, 0))],
        out_specs=tok(d),
        scratch_shapes=[pltpu.VMEM((rows, HEAD), BF16), pltpu.VMEM((rows, ps * n_heads), F32),
                        pltpu.VMEM((rows, 1), F32), pltpu.VMEM((rows, 1), F32),
                        pltpu.VMEM((rows, HEAD), F32)],
    )
    return pl.pallas_call(
        _fox_paged_kernel,
        grid_spec=grid_spec,
        out_shape=jax.ShapeDtypeStruct(q.shape, F32),
        compiler_params=_params("arbitrary", "arbitrary"),
        name="fox_paged",
    )(pt, q, k, v, lf, cache_k, cache_v, suffix)


def _fox_out(grp, o, g, x, wo):
    d = x.shape[1]
    return pl.pallas_call(
        _fox_out_kernel,
        grid=(grp.n_tiles,),
        in_specs=[grp.row_spec(d)] * 3 + [_whole(wo)],
        out_specs=grp.row_spec(d),
        out_shape=jax.ShapeDtypeStruct(x.shape, F32),
        compiler_params=_params("arbitrary"),
        name="fox_out",
    )(o, g, x, wo)


def _conv_mix(grp, x, state, ng, win, cw, wo):
    d = x.shape[1]
    ins, specs = [x], [grp.row_spec(d)]
    if grp.short:
        ins += [grp.fix_rows(state, 1), grp.fix_rows(state, 2)]
        specs += [grp.row_spec(d)] * 2
    consts = [_row2(ng), win, cw.astype(F32), wo]
    return pl.pallas_call(
        functools.partial(_conv_mix_kernel, tiles_per_seq=grp.tiles_per_seq,
                          seq_rows=grp.seq_rows),
        grid=(grp.n_tiles,),
        in_specs=specs + [_whole(a) for a in consts],
        out_specs=[grp.row_spec(d), grp.tail_spec(d)],
        out_shape=[jax.ShapeDtypeStruct(x.shape, F32),
                   jax.ShapeDtypeStruct((grp.n_tiles * grp.tail_rows, d), F32)],
        scratch_shapes=[pltpu.VMEM((grp.tm + 2 * SUBLANES, d), F32)],
        compiler_params=_params("arbitrary"),
        name="conv_mix",
    )(*ins, *consts)


def _ffn(grp, x, state, ng, wup, cw, cb, wdn):
    d = x.shape[1]
    dff = cw.shape[1]
    n_chunks = 2 if dff % (2 * LANES) == 0 else 1
    ins, specs = [x], [grp.row_spec(d)]
    if grp.short:
        ins += [grp.fix_rows(state, 1), grp.fix_rows(state, 2)]
        specs += [grp.row_spec(dff)] * 2
    consts = [_row2(ng), wup, cw.astype(F32), _row2(cb), wdn]
    return pl.pallas_call(
        functools.partial(_ffn_kernel, tiles_per_seq=grp.tiles_per_seq, seq_rows=grp.seq_rows,
                          n_chunks=n_chunks),
        grid=(grp.n_tiles,),
        in_specs=specs + [_whole(a) for a in consts],
        out_specs=[grp.row_spec(d), grp.tail_spec(dff)],
        out_shape=[jax.ShapeDtypeStruct(x.shape, F32),
                   jax.ShapeDtypeStruct((grp.n_tiles * grp.tail_rows, dff), F32)],
        scratch_shapes=[pltpu.VMEM((grp.tm + 2 * SUBLANES, dff), F32)],
        compiler_params=_params("arbitrary"),
        name="conv_ffn",
    )(*ins, *consts)


def kernel(x_prompt, x_sample, state_wkv, state_shift, cache_k, cache_v, cache_logf, state_conv_mix, state_conv_ffn, page_table, norm_mix_g, norm_ffn_g, a_mu, a_w_rkv, a_w0, a_w1, a_w2, a_a0, a_a1, a_a2, a_g1, a_g2, a_k_k, a_k_a, a_r_k, a_lnx_g, a_lnx_b, a_w_out, b_w_in, b_f_bias, b_q_norm_g, b_k_norm_g, b_w_out, c_w_in, c_conv_w, c_w_out, f_w_up, f_conv_w, f_conv_b, f_w_down):
    bp, tp, d = x_prompt.shape
    db, ts, _ = x_sample.shape
    depth = norm_mix_g.shape[0]
    n_heads = d // HEAD
    assert d % LANES == 0 and c_conv_w.shape[1] == CONV_TAPS and f_conv_w.shape[1] == CONV_TAPS
    gp = _Group(bp, tp, ROW_TILE)
    gs = _Group(db, ts, ROW_TILE)
    groups = (gp, gs)
    xs = [x_prompt.reshape(gp.rows, d), x_sample.reshape(gs.rows, d)]
    bf = lambda w: w.astype(BF16)
    chunk_p = WKV_CHUNK if tp % WKV_CHUNK == 0 else tp
    assert chunk_p <= WKV_CHUNK

    wkv_o, shift_o = ([], []), ([], [])
    k_o, v_o, lf_o = ([], []), ([], []), ([], [])
    cm_o, cf_o = ([], []), ([], [])
    for i in range(depth):
        kind, j = i % 3, i // 3
        if kind == 0:
            weights = [bf(a_w_rkv[j, 0]), bf(a_w_rkv[j, 1]), bf(a_w_rkv[j, 2]), bf(a_w1[j]),
                       bf(a_w2[j]), bf(a_a1[j]), bf(a_a2[j]), bf(a_g1[j]), bf(a_g2[j])]
            wo = bf(a_w_out[j])
            for gi, grp in enumerate(groups):
                shift = state_shift[j] if gi == 1 else None
                r, k, v, lw, kk, b, g, xn = _rwkv_proj(
                    grp, xs[gi], shift, norm_mix_g[i], a_mu[j], a_w0[j], a_a0[j], a_k_k[j],
                    a_k_a[j], weights)
                if gi == 0:
                    s0 = jnp.zeros((grp.n_seq, d // LANES, LANES, LANES), F32)
                    chunk = chunk_p
                else:
                    s0 = _pair_state(state_wkv[j])
                    chunk = ts
                o, s_new = _wkv(grp.n_seq, grp.seq_rows, chunk, r, k, v, lw, kk, b, s0)
                xs[gi] = _rwkv_out(grp, o, r, k, v, g, xs[gi], a_lnx_g[j], a_lnx_b[j],
                                   a_r_k[j].reshape(-1), wo)
                wkv_o[gi].append(_unpair_state(s_new, state_wkv.dtype if gi else F32))
                shift_o[gi].append(xn.reshape(grp.n_seq, grp.seq_rows, d)[:, -1])
        elif kind == 1:
            w_in = b_w_in[j]
            wq, wk, wv, wg = (bf(w_in[:, c * d:(c + 1) * d]) for c in range(4))
            wf = bf(jnp.pad(w_in[:, 4 * d:], ((0, 0), (0, LANES - n_heads))))
            fb = jnp.pad(b_f_bias[j].astype(F32), (0, LANES - n_heads)).reshape(1, LANES)
            qg = jnp.tile(b_q_norm_g[j].astype(F32), n_heads).reshape(1, d)
            kg = jnp.tile(b_k_norm_g[j].astype(F32), n_heads).reshape(1, d)
            wo = bf(b_w_out[j])
            for gi, grp in enumerate(groups):
                q, k, v, g, lf, fc = _fox_proj(grp, xs[gi], norm_mix_g[i], wq, wk, wv, wg, wf,
                                               fb, qg, kg)
                if gi == 0:
                    tile = ATTN_TILE if grp.seq_rows % ATTN_TILE == 0 else grp.tm
                    o = _fox_attn(grp.n_seq, grp.seq_rows, tile, q, k, v, fc)
                else:
                    o = _fox_paged(grp.n_seq, grp.seq_rows, q, k, v, lf, cache_k[j], cache_v[j],
                                   cache_logf[j], page_table)
                xs[gi] = _fox_out(grp, o, g, xs[gi], wo)
                shape = (grp.n_seq, grp.seq_rows, n_heads, HEAD)
                k_o[gi].append(k.reshape(shape))
                v_o[gi].append(v.reshape(shape))
                lf_o[gi].append(lf[:, :n_heads].reshape(shape[:3]))
        else:
            win, wo = bf(c_w_in[j]), bf(c_w_out[j])
            for gi, grp in enumerate(groups):
                state = state_conv_mix[j] if gi == 1 else None
                xs[gi], tail = _conv_mix(grp, xs[gi], state, norm_mix_g[i], win, c_conv_w[j], wo)
                cm_o[gi].append(grp.last_rows(tail, CONV_TAPS - 1))
        wup, wdn = bf(f_w_up[i]), bf(f_w_down[i])
        for gi, grp in enumerate(groups):
            state = state_conv_ffn[i] if gi == 1 else None
            xs[gi], tail = _ffn(grp, xs[gi], state, norm_ffn_g[i], wup, f_conv_w[i],
                                f_conv_b[i], wdn)
            cf_o[gi].append(grp.last_rows(tail, CONV_TAPS - 1))

    st = jnp.stack
    return (xs[0].reshape(bp, tp, d), xs[1].reshape(db, ts, d),
            st(wkv_o[0]), st(shift_o[0]), st(wkv_o[1]), st(shift_o[1]),
            st(k_o[0]), st(v_o[0]), st(lf_o[0]), st(k_o[1]), st(v_o[1]), st(lf_o[1]),
            st(cm_o[0]), st(cm_o[1]), st(cf_o[0]), st(cf_o[1]))
```

```python
import functools

import jax
import jax.numpy as jnp
from jax import lax
from jax.experimental import pallas as pl
from jax.experimental.pallas import tpu as pltpu

HEAD = 64
LANES = 128
SUBLANES = 8
RMS_EPS = 1e-6
GN_EPS = 64e-5
CONV_TAPS = 3
WKV_CHUNK = 64
ROW_TILE = 512
ATTN_TILE = 512
PAGES_PER_STEP = 8
VMEM_LIMIT = 56 * 1024 * 1024

F32 = jnp.float32
BF16 = jnp.bfloat16
_HI = lax.Precision.HIGHEST


def _mm(x, y, precision=None):
    return jnp.dot(x, y, precision=precision, preferred_element_type=F32)


def _nt(x, y, precision=None):
    return lax.dot_general(x, y, (((1,), (1,)), ((), ())), precision=precision,
                           preferred_element_type=F32)


def _tn(x, y, precision=None):
    return lax.dot_general(x, y, (((0,), (0,)), ((), ())), precision=precision,
                           preferred_element_type=F32)


def _iota(shape, dim):
    return lax.broadcasted_iota(jnp.int32, shape, dim)


def _sigmoid(x):
    return 1.0 / (1.0 + jnp.exp(-x))


def _softplus(x):
    return jnp.maximum(x, 0.0) + jnp.log(1.0 + jnp.exp(-jnp.abs(x)))


def _rms(x, g):
    return x * lax.rsqrt(jnp.mean(x * x, axis=-1, keepdims=True) + RMS_EPS) * g


def _pair_ones():
    r = lax.div(_iota((LANES, LANES), 0), HEAD)
    c = lax.div(_iota((LANES, LANES), 1), HEAD)
    return (r == c).astype(F32)


def _head_sums(x, ones):
    parts = [_mm(x[:, i:i + LANES], ones, _HI) for i in range(0, x.shape[1], LANES)]
    return jnp.concatenate(parts, axis=1)


def _stage_rows(sh_ref, cols, u, first_tile):
    tm = u.shape[0]

    @pl.when(first_tile)
    def _():
        sh_ref[0:SUBLANES, cols] = jnp.zeros((SUBLANES, u.shape[1]), u.dtype)

    sh_ref[SUBLANES:SUBLANES + tm, cols] = u


def _prev_rows(sh_ref, cols, tm, k, pos, fix_ref):
    prev = sh_ref[SUBLANES - k:SUBLANES - k + tm, cols]
    if fix_ref is not None:
        prev = jnp.where(pos >= k, prev, fix_ref[:, cols])
    return prev


def _keep_tail(sh_ref, tm):
    sh_ref[0:SUBLANES, :] = sh_ref[tm:tm + SUBLANES, :]


def _seq_pos(tm, seq_rows):
    return lax.rem(_iota((tm, 1), 0), seq_rows)


def _rwkv_proj_kernel(*refs, tiles_per_seq, seq_rows):
    short = seq_rows < refs[0].shape[0]
    x_ref, refs = refs[0], refs[1:]
    fix_ref = None
    if short:
        fix_ref, refs = refs[0], refs[1:]
    (ng_ref, mu_ref, w0_ref, a0_ref, kk_ref, ka_ref,
     wr_ref, wk_ref, wv_ref, w1_ref, w2_ref, a1_ref, a2_ref, g1_ref, g2_ref,
     r_o, k_o, v_o, lw_o, kkn_o, b_o, g_o, xn_o, sh_ref) = refs
    tm = x_ref.shape[0]
    i = pl.program_id(0)
    xn = _rms(x_ref[...], ng_ref[...])
    xn_o[...] = xn
    cols = slice(None)
    _stage_rows(sh_ref, cols, xn, lax.rem(i, tiles_per_seq) == 0)
    pos = _seq_pos(tm, seq_rows) if short else None
    dx = _prev_rows(sh_ref, cols, tm, 1, pos, fix_ref) - xn
    _keep_tail(sh_ref, tm)

    def mix(c):
        return (xn + dx * mu_ref[c:c + 1, :]).astype(BF16)

    r = _mm(mix(0), wr_ref[...])
    k = _mm(mix(2), wk_ref[...])
    v = _mm(mix(3), wv_ref[...])
    wl = w0_ref[...] + _mm(jnp.tanh(_mm(mix(1), w1_ref[...])).astype(BF16), w2_ref[...])
    lw = -jnp.exp(-_softplus(-wl) - 0.5)
    iclr = _sigmoid(a0_ref[...] + _mm(_mm(mix(4), a1_ref[...]).astype(BF16), a2_ref[...]))
    g = _mm(_sigmoid(_mm(mix(5), g1_ref[...])).astype(BF16), g2_ref[...])
    kk = k * kk_ref[...]
    norm = jnp.sqrt(_head_sums(kk * kk, _pair_ones()))
    kk = kk / jnp.maximum(norm, 1e-12)
    r_o[...] = r
    k_o[...] = k * (1.0 + (iclr - 1.0) * ka_ref[...])
    v_o[...] = v
    lw_o[...] = lw
    kkn_o[...] = kk
    b_o[...] = kk * iclr
    g_o[...] = g


def _wkv_kernel(r_ref, k_ref, v_ref, lw_ref, kk_ref, b_ref, s0_ref, o_ref, so_ref, s_ref):
    c = pl.program_id(1)
    chunk, d = r_ref.shape
    n_pairs = d // LANES

    @pl.when(c == 0)
    def _():
        s_ref[...] = s0_ref[...]

    lw = lw_ref[...]
    tril = (_iota((chunk, chunk), 0) >= _iota((chunk, chunk), 1)).astype(F32)
    cum = _mm(tril, lw, _HI)
    e_pos = jnp.exp(cum)
    e_neg = jnp.exp(-cum)
    a_t = -(kk_ref[...] * jnp.exp(cum - lw))
    b_t = b_ref[...] * e_neg
    k_t = k_ref[...] * e_neg
    r_t = r_ref[...] * e_pos
    v_t = v_ref[...]
    g_end = e_pos[chunk - 1:chunk, :]

    first = _iota((chunk, LANES), 1) < HEAD
    rows2 = 2 * chunk
    merged = rows2 % LANES == 0
    ri = lax.rem(_iota((rows2, rows2), 0), chunk)
    ci = lax.rem(_iota((rows2, rows2), 1), chunk)
    strict = ri > ci
    incl = ri >= ci
    eye = (_iota((rows2, rows2), 0) == _iota((rows2, rows2), 1)).astype(F32)

    def stack(x):
        return jnp.concatenate([jnp.where(first, x, 0.0), jnp.where(first, 0.0, x)],
                               axis=0).astype(BF16)

    def scores(ar, bk):
        if merged:
            g = _nt(ar, bk)
            return g[:rows2, :rows2], g[:rows2, rows2:], g[rows2:, :rows2], g[rows2:, rows2:]
        a, r, b, k = ar[:rows2], ar[rows2:], bk[:rows2], bk[rows2:]
        return _nt(a, b), _nt(a, k), _nt(r, b), _nt(r, k)

    pairs = range(n_pairs)
    lanes = [slice(p * LANES, (p + 1) * LANES) for p in pairs]
    ar = [jnp.concatenate([stack(a_t[:, sl]), stack(r_t[:, sl])], axis=0) for sl in lanes]
    bk = [jnp.concatenate([stack(b_t[:, sl]), stack(k_t[:, sl])], axis=0) for sl in lanes]
    vx = [stack(v_t[:, sl]) for sl in lanes]
    sc = [scores(ar[p], bk[p]) for p in pairs]
    pw = [jnp.where(strict, sc[p][0], 0.0) for p in pairs]
    inv = [eye + pw[p] for p in pairs]
    for _ in range(max(chunk.bit_length() - 2, 0)):
        pwb = [pw[p].astype(BF16) for p in pairs]
        pw = [_mm(pwb[p], pwb[p]) for p in pairs]
        inv = [inv[p] + _mm(pw[p].astype(BF16), inv[p].astype(BF16)) for p in pairs]
    for p in pairs:
        l_ak = jnp.where(strict, sc[p][1], 0.0).astype(BF16)
        m_rb = jnp.where(incl, sc[p][2], 0.0).astype(BF16)
        m_rk = jnp.where(incl, sc[p][3], 0.0).astype(BF16)
        s = s_ref[p]
        xs = _nt(ar[p], s.astype(BF16))
        rhs = xs[:rows2] + _mm(l_ak, vx[p])
        u = _mm(inv[p].astype(BF16), rhs.astype(BF16)).astype(BF16)
        uv = jnp.concatenate([u, vx[p]], axis=0)
        if merged:
            oe = xs[rows2:] + _mm(jnp.concatenate([m_rb, m_rk], axis=1), uv)
        else:
            oe = xs[rows2:] + _mm(m_rb, u) + _mm(m_rk, vx[p])
        o_ref[:, lanes[p]] = oe[:chunk] + oe[chunk:]
        s_ref[p] = (s + _tn(uv, bk[p])) * g_end[:, lanes[p]]

    @pl.when(c == pl.num_programs(1) - 1)
    def _():
        so_ref[...] = s_ref[...]


def _rwkv_out_kernel(o_ref, r_ref, k_ref, v_ref, g_ref, x_ref, lng_ref, lnb_ref, rk_ref,
                     wo_ref, y_ref):
    ones = _pair_ones()
    o = o_ref[...]
    d = o - _head_sums(o, ones) * (1.0 / HEAD)
    var = _head_sums(d * d, ones) * (1.0 / HEAD)
    on = d * lax.rsqrt(var + GN_EPS) * lng_ref[...] + lnb_ref[...]
    bonus = _head_sums(r_ref[...] * k_ref[...] * rk_ref[...], ones) * v_ref[...]
    z = ((on + bonus) * g_ref[...]).astype(BF16)
    y_ref[...] = x_ref[...] + _mm(z, wo_ref[...])


def _fox_proj_kernel(x_ref, ng_ref, wq_ref, wk_ref, wv_ref, wg_ref, wf_ref, fb_ref, qg_ref,
                     kg_ref, q_o, k_o, v_o, g_o, lf_o, fc_o, carry_ref, *, tiles_per_seq):
    tm = x_ref.shape[0]
    i = pl.program_id(0)
    xn = _rms(x_ref[...], ng_ref[...]).astype(BF16)
    ones = _pair_ones()
    q = _mm(xn, wq_ref[...])
    q_o[...] = q * lax.rsqrt(_head_sums(q * q, ones) * (1.0 / HEAD) + RMS_EPS) * qg_ref[...]
    k = _mm(xn, wk_ref[...])
    k_o[...] = k * lax.rsqrt(_head_sums(k * k, ones) * (1.0 / HEAD) + RMS_EPS) * kg_ref[...]
    v_o[...] = _mm(xn, wv_ref[...])
    g_o[...] = _mm(xn, wg_ref[...])
    lf = -_softplus(-(_mm(xn, wf_ref[...]) + fb_ref[...]))
    lf_o[...] = lf

    @pl.when(lax.rem(i, tiles_per_seq) == 0)
    def _():
        carry_ref[...] = jnp.zeros(carry_ref.shape, F32)

    tril = (_iota((tm, tm), 0) >= _iota((tm, tm), 1)).astype(F32)
    fc = _mm(tril, lf, _HI) + carry_ref[0:1, :]
    fc_o[...] = fc
    carry_ref[...] = jnp.broadcast_to(fc[tm - 1:tm, :], carry_ref.shape)


def _fox_attn_kernel(q_ref, k_ref, v_ref, f_ref, o_ref, kaug_ref, vt_ref):
    qi = pl.program_id(2)
    n_blocks, tk, _ = k_ref.shape
    tq = q_ref.shape[0]
    lane = _iota((tk, LANES), 1)

    def augment(x, f, h, piece_off, const_off, const):
        rel = lane - HEAD * (1 - h)
        fh = f[:, h:h + 1]
        hi = fh.astype(BF16).astype(F32)
        mid = (fh - hi).astype(BF16).astype(F32)
        lo = fh - hi - mid
        in_const = jnp.where(rel >= const_off, jnp.where(rel < const_off + 3, const, 0.0), 0.0)
        extra = jnp.where(rel == piece_off, hi,
                          jnp.where(rel == piece_off + 1, mid,
                                    jnp.where(rel == piece_off + 2, lo, in_const)))
        own = lax.div(lane, HEAD) == h
        return jnp.where(own, x, extra).astype(BF16)

    @pl.when(qi == 0)
    def _():
        def fill(j, carry):
            kb = k_ref[j]
            f = f_ref[j]
            for h in range(2):
                kaug_ref[h, j] = augment(kb, f, h, 0, 3, 1.0)
            vt_ref[j] = v_ref[j].T.astype(BF16)
            return carry
        lax.fori_loop(0, n_blocks, fill, 0)

    fq = f_ref[qi]
    q = q_ref[...] * (HEAD ** -0.5)
    qa = [augment(q, fq, h, 3, 0, -1.0) for h in range(2)]
    keep = _iota((tk, tq), 0) <= _iota((tk, tq), 1)

    def block(j, carry, diagonal):
        heads = range(2)
        m, l, acc = carry[0::3], carry[1::3], carry[2::3]
        vt = vt_ref[j]
        s = [_nt(kaug_ref[h, j], qa[h]) for h in heads]
        if diagonal:
            s = [jnp.where(keep, s[h], -jnp.inf) for h in heads]
        m_new = [jnp.maximum(m[h], jnp.max(s[h], axis=0, keepdims=True)) for h in heads]
        p = [jnp.exp(s[h] - m_new[h]) for h in heads]
        alpha = [jnp.exp(m[h] - m_new[h]) for h in heads]
        l = [alpha[h] * l[h] + jnp.sum(p[h], axis=0, keepdims=True) for h in heads]
        acc = [alpha[h] * acc[h] + _mm(vt, p[h].astype(BF16)) for h in heads]
        return (m_new[0], l[0], acc[0], m_new[1], l[1], acc[1])

    init = (jnp.full((1, tq), -jnp.inf, F32), jnp.zeros((1, tq), F32),
            jnp.zeros((LANES, tq), F32)) * 2
    carry = lax.fori_loop(0, qi, lambda j, c: block(j, c, False), init)
    m0, l0, acc0, m1, l1, acc1 = block(qi, carry, True)
    top = _iota((LANES, tq), 0) < HEAD
    o_ref[...] = jnp.where(top, acc0 / l0, acc1 / l1).T


def _fox_paged_kernel(pt_ref, q_ref, k_ref, v_ref, lf_ref, *refs, n_sub):
    del pt_ref
    ck_refs, cv_refs, clf_refs = refs[:n_sub], refs[n_sub:2 * n_sub], refs[2 * n_sub:3 * n_sub]
    o_ref, qh_ref, m_ref, l_ref, acc_ref, tot_ref = refs[3 * n_sub:]
    s_idx = pl.program_id(1)
    ts, d = q_ref.shape
    n_heads = d // HEAD
    rows = n_heads * ts

    def online(sc, pv):
        m = m_ref[...]
        m_new = jnp.maximum(m, jnp.max(sc, axis=1, keepdims=True))
        p = jnp.exp(sc - m_new)
        alpha = jnp.exp(m - m_new)
        l_ref[...] = alpha * l_ref[...] + jnp.sum(p, axis=1, keepdims=True)
        acc_ref[...] = alpha * acc_ref[...] + pv(p)
        m_ref[...] = m_new

    @pl.when(s_idx == 0)
    def _():
        q = q_ref[...] * (HEAD ** -0.5)
        for h in range(n_heads):
            qh_ref[h] = q[:, h * HEAD:(h + 1) * HEAD]
        m_ref[...] = jnp.full(m_ref.shape, -jnp.inf, F32)
        l_ref[...] = jnp.zeros(l_ref.shape, F32)
        acc_ref[...] = jnp.zeros(acc_ref.shape, F32)
        tot_ref[...] = jnp.zeros(tot_ref.shape, F32)
        qt = jnp.concatenate([q] * n_heads, axis=0)
        mine = lax.div(_iota((rows, d), 0), ts) == lax.div(_iota((rows, d), 1), HEAD)
        qx = jnp.where(mine, qt, 0.0).astype(BF16)
        lf = lf_ref[...]
        hsel = (lax.div(_iota((rows, lf.shape[1]), 0), ts)
                == _iota((rows, lf.shape[1]), 1)).astype(F32)
        lfx = _nt(hsel, lf, _HI)
        upto = (_iota((ts, ts), 0) <= _iota((ts, ts), 1)).astype(F32)
        cinc = _mm(lfx, upto, _HI)
        tq = lax.rem(_iota((rows, ts), 0), ts)
        sk = _iota((rows, ts), 1)
        sc = _nt(qx, k_ref[...].astype(BF16)) - cinc
        vb = v_ref[...].astype(BF16)

        def pv(p):
            full = _mm(p.astype(BF16), vb)
            return jnp.concatenate(
                [full[h * ts:(h + 1) * ts, h * HEAD:(h + 1) * HEAD] for h in range(n_heads)],
                axis=0)

        online(jnp.where(sk <= tq, sc, -jnp.inf), pv)

    @pl.when(s_idx > 0)
    def _():
        ps = ck_refs[0].shape[2]
        n_keys = n_sub * ps
        heads = (((2,), (1,)), ((0,), (0,)))
        keys = (((2,), (2,)), ((0,), (0,)))
        qh = qh_ref[...].astype(BF16)
        s3 = jnp.concatenate(
            [lax.dot_general(qh, r[...].astype(BF16), heads, preferred_element_type=F32)
             for r in ck_refs], axis=2)
        lf = jnp.concatenate([r[...] for r in clf_refs], axis=0)
        after = (_iota((ps, ps), 0) > _iota((ps, ps), 1)).astype(F32)
        inside = _mm(lf, after, _HI)
        page_sum = jnp.sum(lf, axis=1, keepdims=True)
        tot = tot_ref[...]
        bias = []
        for i in reversed(range(n_sub)):
            bias.append(inside[i * n_heads:(i + 1) * n_heads] + tot)
            tot = tot + page_sum[i * n_heads:(i + 1) * n_heads]
        bias = jnp.concatenate(bias[::-1], axis=1)
        sc = (s3 + bias[:, None, :]).reshape(rows, n_keys)

        def pv(p):
            p3 = p.reshape(n_heads, ts, n_keys).astype(BF16)
            o3 = sum(lax.dot_general(p3[:, :, i * ps:(i + 1) * ps], r[...].astype(BF16), keys,
                                     preferred_element_type=F32) for i, r in enumerate(cv_refs))
            return o3.reshape(rows, HEAD)

        online(sc, pv)
        tot_ref[...] = tot

    @pl.when(s_idx == pl.num_programs(1) - 1)
    def _():
        o = acc_ref[...] / l_ref[...]
        o_ref[...] = jnp.concatenate([o[h * ts:(h + 1) * ts, :] for h in range(n_heads)], axis=1)


def _fox_out_kernel(o_ref, g_ref, x_ref, wo_ref, y_ref):
    z = (o_ref[...] * _sigmoid(g_ref[...])).astype(BF16)
    y_ref[...] = x_ref[...] + _mm(z, wo_ref[...])


def _conv_mix_kernel(*refs, tiles_per_seq, seq_rows):
    short = seq_rows < refs[0].shape[0]
    x_ref, refs = refs[0], refs[1:]
    f1_ref = f2_ref = None
    if short:
        f1_ref, f2_ref, refs = refs[0], refs[1], refs[2:]
    ng_ref, win_ref, cw_ref, wo_ref, y_ref, tail_ref, sh_ref = refs
    tm, d = x_ref.shape
    i = pl.program_id(0)
    x = x_ref[...]
    xn = _rms(x, ng_ref[...]).astype(BF16)
    gb = _mm(xn, win_ref[:, 0:d])
    u = _mm(xn, win_ref[:, d:2 * d]) * _mm(xn, win_ref[:, 2 * d:3 * d])
    cols = slice(None)
    _stage_rows(sh_ref, cols, u, lax.rem(i, tiles_per_seq) == 0)
    pos = _seq_pos(tm, seq_rows) if short else None
    z = (cw_ref[0:1, :] * _prev_rows(sh_ref, cols, tm, 2, pos, f2_ref)
         + cw_ref[1:2, :] * _prev_rows(sh_ref, cols, tm, 1, pos, f1_ref)
         + cw_ref[2:3, :] * u)
    y_ref[...] = x + _mm((gb * z).astype(BF16), wo_ref[...])
    tr = tail_ref.shape[0]
    tail_ref[...] = sh_ref[SUBLANES + tm - tr:SUBLANES + tm, :]
    _keep_tail(sh_ref, tm)


def _ffn_kernel(*refs, tiles_per_seq, seq_rows, n_chunks):
    short = seq_rows < refs[0].shape[0]
    x_ref, refs = refs[0], refs[1:]
    f1_ref = f2_ref = None
    if short:
        f1_ref, f2_ref, refs = refs[0], refs[1], refs[2:]
    ng_ref, wup_ref, cw_ref, cb_ref, wdn_ref, y_ref, tail_ref, sh_ref = refs
    tm = x_ref.shape[0]
    dff = cw_ref.shape[1]
    w = dff // n_chunks
    i = pl.program_id(0)
    x = x_ref[...]
    xn = _rms(x, ng_ref[...]).astype(BF16)
    pos = _seq_pos(tm, seq_rows) if short else None
    first_tile = lax.rem(i, tiles_per_seq) == 0
    acc = x
    for j in range(n_chunks):
        cols = slice(j * w, (j + 1) * w)
        gate = _mm(xn, wup_ref[:, j * w:(j + 1) * w])
        val = _mm(xn, wup_ref[:, dff + j * w:dff + (j + 1) * w])
        _stage_rows(sh_ref, cols, gate, first_tile)
        conv = (cw_ref[0:1, cols] * _prev_rows(sh_ref, cols, tm, 2, pos, f2_ref)
                + cw_ref[1:2, cols] * _prev_rows(sh_ref, cols, tm, 1, pos, f1_ref)
                + cw_ref[2:3, cols] * gate + cb_ref[:, cols])
        h = conv * _sigmoid(conv) * val
        acc = acc + _mm(h.astype(BF16), wdn_ref[j * w:(j + 1) * w, :])
    y_ref[...] = acc
    tr = tail_ref.shape[0]
    tail_ref[...] = sh_ref[SUBLANES + tm - tr:SUBLANES + tm, :]
    _keep_tail(sh_ref, tm)


class _Group:
    def __init__(self, n_seq, seq_rows, max_tile):
        self.n_seq, self.seq_rows = n_seq, seq_rows
        self.rows = n_seq * seq_rows
        if seq_rows % max_tile == 0:
            self.tm = max_tile
        else:
            self.tm = self.rows
            assert seq_rows % SUBLANES == 0 and seq_rows >= CONV_TAPS - 1
        self.short = seq_rows < self.tm
        self.tiles_per_seq = max(seq_rows // self.tm, 1)
        self.n_tiles = self.rows // self.tm
        self.tail_rows = self.tm if self.short else SUBLANES

    def row_spec(self, c):
        return pl.BlockSpec((self.tm, c), lambda i: (i, 0))

    def tail_spec(self, c):
        return pl.BlockSpec((self.tail_rows, c), lambda i: (i, 0))

    def last_rows(self, tail, n):
        c = tail.shape[-1]
        if self.short:
            return tail.reshape(self.n_seq, self.seq_rows, c)[:, self.seq_rows - n:]
        t = tail.reshape(self.n_seq, self.tiles_per_seq, SUBLANES, c)
        return t[:, -1, SUBLANES - n:]

    def fix_rows(self, state, k):
        n_prev, c = state.shape[1], state.shape[2]
        pad = jnp.zeros((self.n_seq, self.seq_rows - k, c), state.dtype)
        return jnp.concatenate([state[:, n_prev - k:], pad], axis=1).reshape(self.rows, c)


def _whole(a):
    nd = a.ndim
    return pl.BlockSpec(a.shape, lambda *_: (0,) * nd, pipeline_mode=pl.Buffered(1))


def _params(*sem):
    return pltpu.CompilerParams(dimension_semantics=sem, vmem_limit_bytes=VMEM_LIMIT)


def _row2(v):
    return v.reshape(1, -1).astype(F32)


def _rwkv_proj(grp, x, shift, ng, mu, w0, a0, k_k, k_a, weights):
    d = x.shape[1]
    vecs = [_row2(ng), mu.astype(F32), _row2(w0), _row2(a0), _row2(k_k), _row2(k_a)]
    ins, specs = [x], [grp.row_spec(d)]
    if grp.short:
        ins.append(grp.fix_rows(shift[:, None, :], 1))
        specs.append(grp.row_spec(d))
    consts = vecs + list(weights)
    out = jax.ShapeDtypeStruct((grp.rows, d), F32)
    return pl.pallas_call(
        functools.partial(_rwkv_proj_kernel, tiles_per_seq=grp.tiles_per_seq,
                          seq_rows=grp.seq_rows),
        grid=(grp.n_tiles,),
        in_specs=specs + [_whole(a) for a in consts],
        out_specs=[grp.row_spec(d)] * 8,
        out_shape=[out] * 8,
        scratch_shapes=[pltpu.VMEM((grp.tm + 2 * SUBLANES, d), F32)],
        compiler_params=_params("arbitrary"),
        name="rwkv_proj",
    )(*ins, *consts)


def _wkv(n_seq, seq_rows, chunk, r, k, v, lw, kk, b, s0):
    d = r.shape[1]
    n_pairs = d // LANES
    n_chunks = seq_rows // chunk
    tok = pl.BlockSpec((chunk, d), lambda bi, c: (bi * n_chunks + c, 0))
    st = pl.BlockSpec((None, n_pairs, LANES, LANES), lambda bi, c: (bi, 0, 0, 0))
    return pl.pallas_call(
        _wkv_kernel,
        grid=(n_seq, n_chunks),
        in_specs=[tok] * 6 + [st],
        out_specs=[tok, st],
        out_shape=[jax.ShapeDtypeStruct(r.shape, F32), jax.ShapeDtypeStruct(s0.shape, F32)],
        scratch_shapes=[pltpu.VMEM((n_pairs, LANES, LANES), F32)],
        compiler_params=_params("arbitrary", "arbitrary"),
        name="wkv",
    )(r, k, v, lw, kk, b, s0)


def _rwkv_out(grp, o, r, k, v, g, x, lng, lnb, rk, wo):
    d = x.shape[1]
    consts = [_row2(lng), _row2(lnb), _row2(rk), wo]
    return pl.pallas_call(
        _rwkv_out_kernel,
        grid=(grp.n_tiles,),
        in_specs=[grp.row_spec(d)] * 6 + [_whole(a) for a in consts],
        out_specs=grp.row_spec(d),
        out_shape=jax.ShapeDtypeStruct(x.shape, F32),
        compiler_params=_params("arbitrary"),
        name="rwkv_out",
    )(o, r, k, v, g, x, *consts)


def _pair_state(s):
    n, h = s.shape[0], s.shape[1]
    s = s.reshape(n, h // 2, 2, HEAD, HEAD).astype(F32)
    z = jnp.zeros_like(s[:, :, 0])
    top = jnp.concatenate([s[:, :, 0], z], axis=-1)
    bot = jnp.concatenate([z, s[:, :, 1]], axis=-1)
    return jnp.concatenate([top, bot], axis=-2)


def _unpair_state(s, dtype):
    n, p = s.shape[0], s.shape[1]
    s = s.reshape(n, p, 2, HEAD, 2, HEAD)
    out = jnp.stack([s[:, :, 0, :, 0, :], s[:, :, 1, :, 1, :]], axis=2)
    return out.reshape(n, 2 * p, HEAD, HEAD).astype(dtype)


def _fox_proj(grp, x, ng, wq, wk, wv, wg, wf, fb, qg, kg):
    d = x.shape[1]
    consts = [_row2(ng), wq, wk, wv, wg, wf, fb, qg, kg]
    wide = jax.ShapeDtypeStruct((grp.rows, d), F32)
    thin = jax.ShapeDtypeStruct((grp.rows, LANES), F32)
    return pl.pallas_call(
        functools.partial(_fox_proj_kernel, tiles_per_seq=grp.tiles_per_seq),
        grid=(grp.n_tiles,),
        in_specs=[grp.row_spec(d)] + [_whole(a) for a in consts],
        out_specs=[grp.row_spec(d)] * 4 + [grp.row_spec(LANES)] * 2,
        out_shape=[wide] * 4 + [thin] * 2,
        scratch_shapes=[pltpu.VMEM((SUBLANES, LANES), F32)],
        compiler_params=_params("arbitrary"),
        name="fox_proj",
    )(x, *consts)


def _fox_attn(n_seq, seq_rows, tq, q, k, v, fcum):
    d = q.shape[1]
    n_pairs = d // LANES
    nq = seq_rows // tq
    h = d // HEAD
    f = fcum[:, :h].reshape(n_seq, nq, tq, n_pairs, 2).transpose(0, 3, 1, 2, 4)
    kv = pl.BlockSpec((None, nq, tq, LANES), lambda b, p, i: (b, 0, 0, p))
    qo = pl.BlockSpec((tq, LANES), lambda b, p, i: (b * nq + i, p))
    return pl.pallas_call(
        _fox_attn_kernel,
        grid=(n_seq, n_pairs, nq),
        in_specs=[qo, kv, kv,
                  pl.BlockSpec((None, None, nq, tq, 2), lambda b, p, i: (b, p, 0, 0, 0))],
        out_specs=qo,
        out_shape=jax.ShapeDtypeStruct(q.shape, F32),
        scratch_shapes=[pltpu.VMEM((2, nq, tq, LANES), BF16), pltpu.VMEM((nq, LANES, tq), BF16)],
        compiler_params=_params("arbitrary", "arbitrary", "arbitrary"),
        name="fox_attn",
    )(q, k.reshape(n_seq, nq, tq, d), v.reshape(n_seq, nq, tq, d), f)


def _fox_paged(n_seq, ts, q, k, v, lf, cache_k, cache_v, cache_lf, page_table):
    d = q.shape[1]
    n_heads = d // HEAD
    rows = n_heads * ts
    ps = cache_k.shape[1]
    n_pages = page_table.shape[1]
    ck = jnp.transpose(cache_k, (0, 2, 3, 1))
    cv = jnp.transpose(cache_v, (0, 2, 3, 1))
    clf = jnp.transpose(cache_lf.astype(F32), (0, 2, 1))

    n_sub = next(n for n in (PAGES_PER_STEP, 2, 1) if n_pages % n == 0)

    def slot(i):
        return lambda b, s, pt: pt[b * n_pages + n_pages - n_sub * jnp.maximum(s, 1) + i]

    tok = lambda c: pl.BlockSpec((ts, c), lambda b, s, pt: (b, 0))
    cache = [pl.BlockSpec((None, n_heads, HEAD, ps),
                          lambda b, s, pt, f=slot(i): (f(b, s, pt), 0, 0, 0)) for i in range(n_sub)]
    gates = [pl.BlockSpec((None, n_heads, ps),
                          lambda b, s, pt, f=slot(i): (f(b, s, pt), 0, 0)) for i in range(n_sub)]
    grid_spec = pltpu.PrefetchScalarGridSpec(
        num_scalar_prefetch=1,
        grid=(n_seq, n_pages // n_sub + 1),
        in_specs=[tok(d), tok(d), tok(d), tok(LANES)] + cache + cache + gates,
        out_specs=tok(d),
        scratch_shapes=[pltpu.VMEM((n_heads, ts, HEAD), F32), pltpu.VMEM((rows, 1), F32),
                        pltpu.VMEM((rows, 1), F32), pltpu.VMEM((rows, HEAD), F32),
                        pltpu.VMEM((n_heads, 1), F32)],
    )
    return pl.pallas_call(
        functools.partial(_fox_paged_kernel, n_sub=n_sub),
        grid_spec=grid_spec,
        out_shape=jax.ShapeDtypeStruct(q.shape, F32),
        compiler_params=_params("arbitrary", "arbitrary"),
        name="fox_paged",
    )(page_table.reshape(-1).astype(jnp.int32), q, k, v, lf,
      *([ck] * n_sub), *([cv] * n_sub), *([clf] * n_sub))


def _fox_out(grp, o, g, x, wo):
    d = x.shape[1]
    return pl.pallas_call(
        _fox_out_kernel,
        grid=(grp.n_tiles,),
        in_specs=[grp.row_spec(d)] * 3 + [_whole(wo)],
        out_specs=grp.row_spec(d),
        out_shape=jax.ShapeDtypeStruct(x.shape, F32),
        compiler_params=_params("arbitrary"),
        name="fox_out",
    )(o, g, x, wo)


def _conv_mix(grp, x, state, ng, win, cw, wo):
    d = x.shape[1]
    ins, specs = [x], [grp.row_spec(d)]
    if grp.short:
        ins += [grp.fix_rows(state, 1), grp.fix_rows(state, 2)]
        specs += [grp.row_spec(d)] * 2
    consts = [_row2(ng), win, cw.astype(F32), wo]
    return pl.pallas_call(
        functools.partial(_conv_mix_kernel, tiles_per_seq=grp.tiles_per_seq,
                          seq_rows=grp.seq_rows),
        grid=(grp.n_tiles,),
        in_specs=specs + [_whole(a) for a in consts],
        out_specs=[grp.row_spec(d), grp.tail_spec(d)],
        out_shape=[jax.ShapeDtypeStruct(x.shape, F32),
                   jax.ShapeDtypeStruct((grp.n_tiles * grp.tail_rows, d), F32)],
        scratch_shapes=[pltpu.VMEM((grp.tm + 2 * SUBLANES, d), F32)],
        compiler_params=_params("arbitrary"),
        name="conv_mix",
    )(*ins, *consts)


def _ffn(grp, x, state, ng, wup, cw, cb, wdn):
    d = x.shape[1]
    dff = cw.shape[1]
    n_chunks = 2 if dff % (2 * LANES) == 0 else 1
    ins, specs = [x], [grp.row_spec(d)]
    if grp.short:
        ins += [grp.fix_rows(state, 1), grp.fix_rows(state, 2)]
        specs += [grp.row_spec(dff)] * 2
    consts = [_row2(ng), wup, cw.astype(F32), _row2(cb), wdn]
    return pl.pallas_call(
        functools.partial(_ffn_kernel, tiles_per_seq=grp.tiles_per_seq, seq_rows=grp.seq_rows,
                          n_chunks=n_chunks),
        grid=(grp.n_tiles,),
        in_specs=specs + [_whole(a) for a in consts],
        out_specs=[grp.row_spec(d), grp.tail_spec(dff)],
        out_shape=[jax.ShapeDtypeStruct(x.shape, F32),
                   jax.ShapeDtypeStruct((grp.n_tiles * grp.tail_rows, dff), F32)],
        scratch_shapes=[pltpu.VMEM((grp.tm + 2 * SUBLANES, dff), F32)],
        compiler_params=_params("arbitrary"),
        name="conv_ffn",
    )(*ins, *consts)


def kernel(x_prompt, x_sample, state_wkv, state_shift, cache_k, cache_v, cache_logf, state_conv_mix, state_conv_ffn, page_table, norm_mix_g, norm_ffn_g, a_mu, a_w_rkv, a_w0, a_w1, a_w2, a_a0, a_a1, a_a2, a_g1, a_g2, a_k_k, a_k_a, a_r_k, a_lnx_g, a_lnx_b, a_w_out, b_w_in, b_f_bias, b_q_norm_g, b_k_norm_g, b_w_out, c_w_in, c_conv_w, c_w_out, f_w_up, f_conv_w, f_conv_b, f_w_down):
    bp, tp, d = x_prompt.shape
    db, ts, _ = x_sample.shape
    depth = norm_mix_g.shape[0]
    n_heads = d // HEAD
    assert d % LANES == 0 and c_conv_w.shape[1] == CONV_TAPS and f_conv_w.shape[1] == CONV_TAPS
    gp = _Group(bp, tp, ROW_TILE)
    gs = _Group(db, ts, ROW_TILE)
    groups = (gp, gs)
    xs = [x_prompt.reshape(gp.rows, d), x_sample.reshape(gs.rows, d)]
    bf = lambda w: w.astype(BF16)
    chunk_p = WKV_CHUNK if tp % WKV_CHUNK == 0 else tp
    assert chunk_p <= WKV_CHUNK

    wkv_o, shift_o = ([], []), ([], [])
    k_o, v_o, lf_o = ([], []), ([], []), ([], [])
    cm_o, cf_o = ([], []), ([], [])
    for i in range(depth):
        kind, j = i % 3, i // 3
        if kind == 0:
            weights = [bf(a_w_rkv[j, 0]), bf(a_w_rkv[j, 1]), bf(a_w_rkv[j, 2]), bf(a_w1[j]),
                       bf(a_w2[j]), bf(a_a1[j]), bf(a_a2[j]), bf(a_g1[j]), bf(a_g2[j])]
            wo = bf(a_w_out[j])
            for gi, grp in enumerate(groups):
                shift = state_shift[j] if gi == 1 else None
                r, k, v, lw, kk, b, g, xn = _rwkv_proj(
                    grp, xs[gi], shift, norm_mix_g[i], a_mu[j], a_w0[j], a_a0[j], a_k_k[j],
                    a_k_a[j], weights)
                if gi == 0:
                    s0 = jnp.zeros((grp.n_seq, d // LANES, LANES, LANES), F32)
                    chunk = chunk_p
                else:
                    s0 = _pair_state(state_wkv[j])
                    chunk = ts
                o, s_new = _wkv(grp.n_seq, grp.seq_rows, chunk, r, k, v, lw, kk, b, s0)
                xs[gi] = _rwkv_out(grp, o, r, k, v, g, xs[gi], a_lnx_g[j], a_lnx_b[j],
                                   a_r_k[j].reshape(-1), wo)
                wkv_o[gi].append(_unpair_state(s_new, state_wkv.dtype if gi else F32))
                shift_o[gi].append(xn.reshape(grp.n_seq, grp.seq_rows, d)[:, -1])
        elif kind == 1:
            w_in = b_w_in[j]
            wq, wk, wv, wg = (bf(w_in[:, c * d:(c + 1) * d]) for c in range(4))
            wf = bf(jnp.pad(w_in[:, 4 * d:], ((0, 0), (0, LANES - n_heads))))
            fb = jnp.pad(b_f_bias[j].astype(F32), (0, LANES - n_heads)).reshape(1, LANES)
            qg = jnp.tile(b_q_norm_g[j].astype(F32), n_heads).reshape(1, d)
            kg = jnp.tile(b_k_norm_g[j].astype(F32), n_heads).reshape(1, d)
            wo = bf(b_w_out[j])
            for gi, grp in enumerate(groups):
                q, k, v, g, lf, fc = _fox_proj(grp, xs[gi], norm_mix_g[i], wq, wk, wv, wg, wf,
                                               fb, qg, kg)
                if gi == 0:
                    tile = ATTN_TILE if grp.seq_rows % ATTN_TILE == 0 else grp.tm
                    o = _fox_attn(grp.n_seq, grp.seq_rows, tile, q, k, v, fc)
                else:
                    o = _fox_paged(grp.n_seq, grp.seq_rows, q, k, v, lf, cache_k[j], cache_v[j],
                                   cache_logf[j], page_table)
                xs[gi] = _fox_out(grp, o, g, xs[gi], wo)
                shape = (grp.n_seq, grp.seq_rows, n_heads, HEAD)
                k_o[gi].append(k.reshape(shape))
                v_o[gi].append(v.reshape(shape))
                lf_o[gi].append(lf[:, :n_heads].reshape(shape[:3]))
        else:
            win, wo = bf(c_w_in[j]), bf(c_w_out[j])
            for gi, grp in enumerate(groups):
                state = state_conv_mix[j] if gi == 1 else None
                xs[gi], tail = _conv_mix(grp, xs[gi], state, norm_mix_g[i], win, c_conv_w[j], wo)
                cm_o[gi].append(grp.last_rows(tail, CONV_TAPS - 1))
        wup, wdn = bf(f_w_up[i]), bf(f_w_down[i])
        for gi, grp in enumerate(groups):
            state = state_conv_ffn[i] if gi == 1 else None
            xs[gi], tail = _ffn(grp, xs[gi], state, norm_ffn_g[i], wup, f_conv_w[i],
                                f_conv_b[i], wdn)
            cf_o[gi].append(grp.last_rows(tail, CONV_TAPS - 1))

    st = jnp.stack
    return (xs[0].reshape(bp, tp, d), xs[1].reshape(db, ts, d),
            st(wkv_o[0]), st(shift_o[0]), st(wkv_o[1]), st(shift_o[1]),
            st(k_o[0]), st(v_o[0]), st(lf_o[0]), st(k_o[1]), st(v_o[1]), st(lf_o[1]),
            st(cm_o[0]), st(cm_o[1]), st(cf_o[0]), st(cf_o[1]))
```

```python
import functools

import jax
import jax.numpy as jnp
from jax import lax
from jax.experimental import pallas as pl
from jax.experimental.pallas import tpu as pltpu

HEAD = 64
LANES = 128
SUBLANES = 8
LOG2_E = 1.4426950408889634
RMS_EPS = 1e-6
GN_EPS = 64e-5
CONV_TAPS = 3
WKV_CHUNK = 64
WKV_STEP = 128
ROW_TILE = 512
ATTN_TILE = 512
PAGES_PER_STEP = 8
VMEM_LIMIT = 56 * 1024 * 1024

F32 = jnp.float32
BF16 = jnp.bfloat16
_HI = lax.Precision.HIGHEST


def _mm(x, y, precision=None):
    return jnp.dot(x, y, precision=precision, preferred_element_type=F32)


def _nt(x, y, precision=None):
    return lax.dot_general(x, y, (((1,), (1,)), ((), ())), precision=precision,
                           preferred_element_type=F32)


def _tn(x, y, precision=None):
    return lax.dot_general(x, y, (((0,), (0,)), ((), ())), precision=precision,
                           preferred_element_type=F32)


def _iota(shape, dim):
    return lax.broadcasted_iota(jnp.int32, shape, dim)


def _sigmoid(x):
    return 1.0 / (1.0 + jnp.exp(-x))


def _softplus(x):
    return jnp.maximum(x, 0.0) + jnp.log(1.0 + jnp.exp(-jnp.abs(x)))


def _rms(x, g):
    return x * lax.rsqrt(jnp.mean(x * x, axis=-1, keepdims=True) + RMS_EPS) * g


def _pair_ones():
    r = lax.div(_iota((LANES, LANES), 0), HEAD)
    c = lax.div(_iota((LANES, LANES), 1), HEAD)
    return (r == c).astype(BF16)


def _head_sums(x, ones):
    hi = x.astype(BF16)
    lo = (x - hi.astype(F32)).astype(BF16)
    parts = [_mm(hi[:, i:i + LANES], ones) + _mm(lo[:, i:i + LANES], ones)
             for i in range(0, x.shape[1], LANES)]
    return jnp.concatenate(parts, axis=1)


def _stage_rows(sh_ref, cols, u, first_tile):
    tm = u.shape[0]

    @pl.when(first_tile)
    def _():
        sh_ref[0:SUBLANES, cols] = jnp.zeros((SUBLANES, u.shape[1]), u.dtype)

    sh_ref[SUBLANES:SUBLANES + tm, cols] = u


def _prev_rows(sh_ref, cols, tm, k, pos, fix_ref):
    prev = sh_ref[SUBLANES - k:SUBLANES - k + tm, cols]
    if fix_ref is not None:
        prev = jnp.where(pos >= k, prev, fix_ref[:, cols])
    return prev


def _keep_tail(sh_ref, tm):
    sh_ref[0:SUBLANES, :] = sh_ref[tm:tm + SUBLANES, :]


def _seq_pos(tm, seq_rows):
    return lax.rem(_iota((tm, 1), 0), seq_rows)


def _rwkv_proj_kernel(*refs, tiles_per_seq, seq_rows):
    short = seq_rows < refs[0].shape[0]
    x_ref, refs = refs[0], refs[1:]
    fix_ref = None
    if short:
        fix_ref, refs = refs[0], refs[1:]
    (ng_ref, mu_ref, w0_ref, a0_ref, kk_ref, ka_ref,
     wr_ref, wk_ref, wv_ref, w1_ref, w2_ref, a1_ref, a2_ref, g1_ref, g2_ref,
     r_o, k_o, v_o, lw_o, kkn_o, b_o, g_o, xn_o, sh_ref) = refs
    tm = x_ref.shape[0]
    i = pl.program_id(0)
    xn = _rms(x_ref[...], ng_ref[...])
    xn_o[...] = xn
    cols = slice(None)
    _stage_rows(sh_ref, cols, xn, lax.rem(i, tiles_per_seq) == 0)
    pos = _seq_pos(tm, seq_rows) if short else None
    dx = _prev_rows(sh_ref, cols, tm, 1, pos, fix_ref) - xn
    _keep_tail(sh_ref, tm)

    def mix(c):
        return (xn + dx * mu_ref[c:c + 1, :]).astype(BF16)

    r = _mm(mix(0), wr_ref[...])
    k = _mm(mix(2), wk_ref[...])
    v = _mm(mix(3), wv_ref[...])
    wl = w0_ref[...] + _mm(jnp.tanh(_mm(mix(1), w1_ref[...])).astype(BF16), w2_ref[...])
    lw = -jnp.exp(-_softplus(-wl) - 0.5)
    iclr = _sigmoid(a0_ref[...] + _mm(_mm(mix(4), a1_ref[...]).astype(BF16), a2_ref[...]))
    g = _mm(_sigmoid(_mm(mix(5), g1_ref[...])).astype(BF16), g2_ref[...])
    kk = k * kk_ref[...]
    norm = jnp.sqrt(_head_sums(kk * kk, _pair_ones()))
    kk = kk / jnp.maximum(norm, 1e-12)
    r_o[...] = r
    k_o[...] = k * (1.0 + (iclr - 1.0) * ka_ref[...])
    v_o[...] = v
    lw_o[...] = lw
    kkn_o[...] = kk
    b_o[...] = kk * iclr
    g_o[...] = g


def _wkv_kernel(r_ref, k_ref, v_ref, lw_ref, kk_ref, b_ref, s0_ref, o_ref, so_ref, s_ref, *,
                chunk):
    c = pl.program_id(1)
    rows, d = r_ref.shape
    n_pairs = d // LANES

    @pl.when(c == 0)
    def _():
        zero = jnp.zeros((HEAD, HEAD), F32)
        for p in range(n_pairs):
            s_ref[p] = jnp.concatenate(
                [jnp.concatenate([s0_ref[2 * p], zero], axis=1),
                 jnp.concatenate([zero, s0_ref[2 * p + 1]], axis=1)], axis=0)

    tril = (_iota((chunk, chunk), 0) >= _iota((chunk, chunk), 1)).astype(F32)
    first = _iota((chunk, LANES), 1) < HEAD
    rows2 = 2 * chunk
    merged = rows2 % LANES == 0
    ri = lax.rem(_iota((rows2, rows2), 0), chunk)
    ci = lax.rem(_iota((rows2, rows2), 1), chunk)
    strict = ri > ci
    incl = ri >= ci
    eye = (_iota((rows2, rows2), 0) == _iota((rows2, rows2), 1)).astype(F32)
    lanes = [slice(p * LANES, (p + 1) * LANES) for p in range(n_pairs)]

    def stack(x):
        return jnp.concatenate([jnp.where(first, x, 0.0), jnp.where(first, 0.0, x)],
                               axis=0).astype(BF16)

    def scores(ar, bk):
        if merged:
            g = _nt(ar, bk)
            return g[:rows2, :rows2], g[:rows2, rows2:], g[rows2:, :rows2], g[rows2:, rows2:]
        a, r, b, k = ar[:rows2], ar[rows2:], bk[:rows2], bk[rows2:]
        return _nt(a, b), _nt(a, k), _nt(r, b), _nt(r, k)

    ar, bk, vx, sc, g_end, tok = [], [], [], [], [], []
    for t0 in range(0, rows, chunk):
        ts = slice(t0, t0 + chunk)
        lw = lw_ref[ts, :]
        cum = _mm(tril, lw, _HI)
        e_pos = jnp.exp(cum)
        e_neg = jnp.exp(-cum)
        a_t = -(kk_ref[ts, :] * jnp.exp(cum - lw))
        b_t = b_ref[ts, :] * e_neg
        k_t = k_ref[ts, :] * e_neg
        r_t = r_ref[ts, :] * e_pos
        v_t = v_ref[ts, :]
        for sl in lanes:
            ar.append(jnp.concatenate([stack(a_t[:, sl]), stack(r_t[:, sl])], axis=0))
            bk.append(jnp.concatenate([stack(b_t[:, sl]), stack(k_t[:, sl])], axis=0))
            vx.append(stack(v_t[:, sl]))
            sc.append(scores(ar[-1], bk[-1]))
            g_end.append(e_pos[chunk - 1:chunk, sl])
            tok.append(ts)
    items = range(len(ar))
    pw = [jnp.where(strict, sc[i][0], 0.0) for i in items]
    inv = [eye + pw[i] for i in items]
    for _ in range(max(chunk.bit_length() - 2, 0)):
        pwb = [pw[i].astype(BF16) for i in items]
        pw = [_mm(pwb[i], pwb[i]) for i in items]
        inv = [inv[i] + _mm(pw[i].astype(BF16), inv[i].astype(BF16)) for i in items]
    invb = [inv[i].astype(BF16) for i in items]
    w = [_mm(invb[i], ar[i][:rows2]).astype(BF16) for i in items]
    uv0 = [_mm(invb[i], _mm(jnp.where(strict, sc[i][1], 0.0).astype(BF16), vx[i]).astype(BF16))
           for i in items]
    for i in items:
        p = i % n_pairs
        m_rb = jnp.where(incl, sc[i][2], 0.0).astype(BF16)
        m_rk = jnp.where(incl, sc[i][3], 0.0).astype(BF16)
        s = s_ref[p]
        sb = s.astype(BF16)
        u = (_nt(w[i], sb) + uv0[i]).astype(BF16)
        uv = jnp.concatenate([u, vx[i]], axis=0)
        s_ref[p] = (s + _tn(uv, bk[i])) * g_end[i]
        if merged:
            oe = _nt(ar[i][rows2:], sb) + _mm(jnp.concatenate([m_rb, m_rk], axis=1), uv)
        else:
            oe = _nt(ar[i][rows2:], sb) + _mm(m_rb, u) + _mm(m_rk, vx[i])
        o_ref[tok[i], lanes[p]] = oe[:chunk] + oe[chunk:]

    @pl.when(c == pl.num_programs(1) - 1)
    def _():
        for p in range(n_pairs):
            s = s_ref[p]
            so_ref[2 * p] = s[:HEAD, :HEAD]
            so_ref[2 * p + 1] = s[HEAD:, HEAD:]


def _rwkv_out_kernel(o_ref, r_ref, k_ref, v_ref, g_ref, x_ref, lng_ref, lnb_ref, rk_ref,
                     wo_ref, y_ref):
    ones = _pair_ones()
    o = o_ref[...]
    d = o - _head_sums(o, ones) * (1.0 / HEAD)
    var = _head_sums(d * d, ones) * (1.0 / HEAD)
    on = d * lax.rsqrt(var + GN_EPS) * lng_ref[...] + lnb_ref[...]
    bonus = _head_sums(r_ref[...] * k_ref[...] * rk_ref[...], ones) * v_ref[...]
    z = ((on + bonus) * g_ref[...]).astype(BF16)
    y_ref[...] = x_ref[...] + _mm(z, wo_ref[...])


def _fox_proj_kernel(x_ref, ng_ref, wq_ref, wk_ref, wv_ref, wg_ref, wf_ref, fb_ref, qg_ref,
                     kg_ref, q_o, k_o, v_o, g_o, lf_o, fc_o, *rest, tiles_per_seq):
    tm = x_ref.shape[0]
    i = pl.program_id(0)
    xn = _rms(x_ref[...], ng_ref[...]).astype(BF16)
    ones = _pair_ones()
    q = _mm(xn, wq_ref[...])
    q_o[...] = q * lax.rsqrt(_head_sums(q * q, ones) * (1.0 / HEAD) + RMS_EPS) * qg_ref[...]
    k = _mm(xn, wk_ref[...])
    k = k * lax.rsqrt(_head_sums(k * k, ones) * (1.0 / HEAD) + RMS_EPS) * kg_ref[...]
    k_o[...] = k
    v = _mm(xn, wv_ref[...])
    v_o[...] = v
    if len(rest) == 3:
        kt_o, vt_o, carry_ref = rest
        kt_o[...] = k.T
        vt_o[...] = v.T
    else:
        (carry_ref,) = rest
    g_o[...] = _mm(xn, wg_ref[...])
    lf = -_softplus(-(_mm(xn, wf_ref[...]) + fb_ref[...]))
    lf_o[...] = lf

    @pl.when(lax.rem(i, tiles_per_seq) == 0)
    def _():
        carry_ref[...] = jnp.zeros(carry_ref.shape, F32)

    tril = (_iota((tm, tm), 0) >= _iota((tm, tm), 1)).astype(F32)
    fc = _mm(tril, lf, _HI) + carry_ref[0:1, :]
    fc_o[...] = fc
    carry_ref[...] = jnp.broadcast_to(fc[tm - 1:tm, :], carry_ref.shape)


def _fox_attn_kernel(q_ref, k_ref, v_ref, f_ref, o_ref, kaug_ref, vt_ref):
    qi = pl.program_id(2)
    n_blocks, tk, _ = k_ref.shape
    tq = q_ref.shape[0]
    lane = _iota((tk, LANES), 1)

    def augment(x, f, h, piece_off, const_off, const):
        rel = lane - HEAD * (1 - h)
        fh = f[:, h:h + 1] * LOG2_E
        hi = fh.astype(BF16).astype(F32)
        mid = (fh - hi).astype(BF16).astype(F32)
        lo = fh - hi - mid
        in_const = jnp.where(rel >= const_off, jnp.where(rel < const_off + 3, const, 0.0), 0.0)
        extra = jnp.where(rel == piece_off, hi,
                          jnp.where(rel == piece_off + 1, mid,
                                    jnp.where(rel == piece_off + 2, lo, in_const)))
        own = lax.div(lane, HEAD) == h
        return jnp.where(own, x, extra).astype(BF16)

    @pl.when(qi == 0)
    def _():
        def fill(j, carry):
            kb = k_ref[j]
            f = f_ref[j]
            for h in range(2):
                kaug_ref[h, j] = augment(kb, f, h, 0, 3, 1.0)
            vt_ref[j] = v_ref[j].T.astype(BF16)
            return carry
        lax.fori_loop(0, n_blocks, fill, 0)

    fq = f_ref[qi]
    q = q_ref[...] * (HEAD ** -0.5 * LOG2_E)
    qa = [augment(q, fq, h, 3, 0, -1.0) for h in range(2)]
    keep = _iota((tk, tq), 0) <= _iota((tk, tq), 1)

    def block(j, carry, diagonal):
        heads = range(2)
        m, l, acc = carry[0::3], carry[1::3], carry[2::3]
        vt = vt_ref[j]
        s = [_nt(kaug_ref[h, j], qa[h]) for h in heads]
        if diagonal:
            s = [jnp.where(keep, s[h], -jnp.inf) for h in heads]
        m_new = [jnp.maximum(m[h], jnp.max(s[h], axis=0, keepdims=True)) for h in heads]
        p = [jnp.exp2(s[h] - m_new[h]) for h in heads]
        alpha = [jnp.exp2(m[h] - m_new[h]) for h in heads]
        l = [alpha[h] * l[h] + jnp.sum(p[h], axis=0, keepdims=True) for h in heads]
        acc = [alpha[h] * acc[h] + _mm(vt, p[h].astype(BF16)) for h in heads]
        return (m_new[0], l[0], acc[0], m_new[1], l[1], acc[1])

    init = (jnp.full((1, tq), -jnp.inf, F32), jnp.zeros((1, tq), F32),
            jnp.zeros((LANES, tq), F32)) * 2
    carry = lax.fori_loop(0, qi, lambda j, c: block(j, c, False), init)
    m0, l0, acc0, m1, l1, acc1 = block(qi, carry, True)
    top = _iota((LANES, tq), 0) < HEAD
    o_ref[...] = jnp.where(top, acc0 / l0, acc1 / l1).T


def _fox_paged_kernel(pt_ref, q_ref, k_ref, v_ref, lf_ref, *refs, n_sub):
    del pt_ref
    ck_refs, cv_refs, clf_refs = refs[:n_sub], refs[n_sub:2 * n_sub], refs[2 * n_sub:3 * n_sub]
    o_ref, qh_ref, m_ref, l_ref, acc_ref, tot_ref = refs[3 * n_sub:]
    s_idx = pl.program_id(1)
    ts, d = q_ref.shape
    n_heads = d // HEAD
    rows = n_heads * ts

    def online(sc, pv):
        m = m_ref[...]
        m_new = jnp.maximum(m, jnp.max(sc, axis=1, keepdims=True))
        p = jnp.exp(sc - m_new)
        alpha = jnp.exp(m - m_new)
        l_ref[...] = alpha * l_ref[...] + jnp.sum(p, axis=1, keepdims=True)
        acc_ref[...] = alpha * acc_ref[...] + pv(p)
        m_ref[...] = m_new

    @pl.when(s_idx == 0)
    def _():
        q = q_ref[...] * (HEAD ** -0.5)
        for h in range(n_heads):
            qh_ref[h] = q[:, h * HEAD:(h + 1) * HEAD]
        m_ref[...] = jnp.full(m_ref.shape, -jnp.inf, F32)
        l_ref[...] = jnp.zeros(l_ref.shape, F32)
        acc_ref[...] = jnp.zeros(acc_ref.shape, F32)
        tot_ref[...] = jnp.zeros(tot_ref.shape, F32)
        qt = jnp.concatenate([q] * n_heads, axis=0)
        mine = lax.div(_iota((rows, d), 0), ts) == lax.div(_iota((rows, d), 1), HEAD)
        qx = jnp.where(mine, qt, 0.0).astype(BF16)
        lf = lf_ref[...]
        hsel = (lax.div(_iota((rows, lf.shape[1]), 0), ts)
                == _iota((rows, lf.shape[1]), 1)).astype(F32)
        lfx = _nt(hsel, lf, _HI)
        upto = (_iota((ts, ts), 0) <= _iota((ts, ts), 1)).astype(F32)
        cinc = _mm(lfx, upto, _HI)
        tq = lax.rem(_iota((rows, ts), 0), ts)
        sk = _iota((rows, ts), 1)
        sc = _nt(qx, k_ref[...].astype(BF16)) - cinc
        vb = v_ref[...].astype(BF16)

        def pv(p):
            full = _mm(p.astype(BF16), vb)
            return jnp.concatenate(
                [full[h * ts:(h + 1) * ts, h * HEAD:(h + 1) * HEAD] for h in range(n_heads)],
                axis=0)

        online(jnp.where(sk <= tq, sc, -jnp.inf), pv)

    @pl.when(s_idx > 0)
    def _():
        ps = ck_refs[0].shape[2]
        n_keys = n_sub * ps
        heads = (((2,), (1,)), ((0,), (0,)))
        keys = (((2,), (2,)), ((0,), (0,)))
        qh = qh_ref[...].astype(BF16)
        s3 = jnp.concatenate(
            [lax.dot_general(qh, r[...].astype(BF16), heads, preferred_element_type=F32)
             for r in ck_refs], axis=2)
        lf = jnp.concatenate([r[...] for r in clf_refs], axis=0)
        after = (_iota((ps, ps), 0) > _iota((ps, ps), 1)).astype(F32)
        inside = _mm(lf, after, _HI)
        page_sum = jnp.sum(lf, axis=1, keepdims=True)
        tot = tot_ref[...]
        bias = []
        for i in reversed(range(n_sub)):
            bias.append(inside[i * n_heads:(i + 1) * n_heads] + tot)
            tot = tot + page_sum[i * n_heads:(i + 1) * n_heads]
        bias = jnp.concatenate(bias[::-1], axis=1)
        sc = (s3 + bias[:, None, :]).reshape(rows, n_keys)

        def pv(p):
            p3 = p.reshape(n_heads, ts, n_keys).astype(BF16)
            o3 = sum(lax.dot_general(p3[:, :, i * ps:(i + 1) * ps], r[...].astype(BF16), keys,
                                     preferred_element_type=F32) for i, r in enumerate(cv_refs))
            return o3.reshape(rows, HEAD)

        online(sc, pv)
        tot_ref[...] = tot

    @pl.when(s_idx == pl.num_programs(1) - 1)
    def _():
        o = acc_ref[...] / l_ref[...]
        o_ref[...] = jnp.concatenate([o[h * ts:(h + 1) * ts, :] for h in range(n_heads)], axis=1)


def _fox_out_kernel(o_ref, g_ref, x_ref, wo_ref, y_ref):
    z = (o_ref[...] * _sigmoid(g_ref[...])).astype(BF16)
    y_ref[...] = x_ref[...] + _mm(z, wo_ref[...])


def _conv_mix_kernel(*refs, tiles_per_seq, seq_rows):
    short = seq_rows < refs[0].shape[0]
    x_ref, refs = refs[0], refs[1:]
    f1_ref = f2_ref = None
    if short:
        f1_ref, f2_ref, refs = refs[0], refs[1], refs[2:]
    ng_ref, win_ref, cw_ref, wo_ref, y_ref, tail_ref, sh_ref = refs
    tm, d = x_ref.shape
    i = pl.program_id(0)
    x = x_ref[...]
    xn = _rms(x, ng_ref[...]).astype(BF16)
    gb = _mm(xn, win_ref[:, 0:d])
    u = _mm(xn, win_ref[:, d:2 * d]) * _mm(xn, win_ref[:, 2 * d:3 * d])
    cols = slice(None)
    _stage_rows(sh_ref, cols, u, lax.rem(i, tiles_per_seq) == 0)
    pos = _seq_pos(tm, seq_rows) if short else None
    z = (cw_ref[0:1, :] * _prev_rows(sh_ref, cols, tm, 2, pos, f2_ref)
         + cw_ref[1:2, :] * _prev_rows(sh_ref, cols, tm, 1, pos, f1_ref)
         + cw_ref[2:3, :] * u)
    y_ref[...] = x + _mm((gb * z).astype(BF16), wo_ref[...])
    tr = tail_ref.shape[0]
    tail_ref[...] = sh_ref[SUBLANES + tm - tr:SUBLANES + tm, :]
    _keep_tail(sh_ref, tm)


def _ffn_kernel(*refs, tiles_per_seq, seq_rows, n_chunks):
    short = seq_rows < refs[0].shape[0]
    x_ref, refs = refs[0], refs[1:]
    f1_ref = f2_ref = None
    if short:
        f1_ref, f2_ref, refs = refs[0], refs[1], refs[2:]
    ng_ref, wup_ref, cw_ref, cb_ref, wdn_ref, y_ref, tail_ref, sh_ref = refs
    tm = x_ref.shape[0]
    dff = cw_ref.shape[1]
    w = dff // n_chunks
    i = pl.program_id(0)
    x = x_ref[...]
    xn = _rms(x, ng_ref[...]).astype(BF16)
    pos = _seq_pos(tm, seq_rows) if short else None
    first_tile = lax.rem(i, tiles_per_seq) == 0
    acc = x
    for j in range(n_chunks):
        cols = slice(j * w, (j + 1) * w)
        gate = _mm(xn, wup_ref[:, j * w:(j + 1) * w])
        val = _mm(xn, wup_ref[:, dff + j * w:dff + (j + 1) * w])
        _stage_rows(sh_ref, cols, gate, first_tile)
        conv = (cw_ref[0:1, cols] * _prev_rows(sh_ref, cols, tm, 2, pos, f2_ref)
                + cw_ref[1:2, cols] * _prev_rows(sh_ref, cols, tm, 1, pos, f1_ref)
                + cw_ref[2:3, cols] * gate + cb_ref[:, cols])
        h = conv * _sigmoid(conv) * val
        acc = acc + _mm(h.astype(BF16), wdn_ref[j * w:(j + 1) * w, :])
    y_ref[...] = acc
    tr = tail_ref.shape[0]
    tail_ref[...] = sh_ref[SUBLANES + tm - tr:SUBLANES + tm, :]
    _keep_tail(sh_ref, tm)


class _Group:
    def __init__(self, n_seq, seq_rows, max_tile):
        self.n_seq, self.seq_rows = n_seq, seq_rows
        self.rows = n_seq * seq_rows
        if seq_rows % max_tile == 0:
            self.tm = max_tile
        else:
            self.tm = self.rows
            assert seq_rows % SUBLANES == 0 and seq_rows >= CONV_TAPS - 1
        self.short = seq_rows < self.tm
        self.tiles_per_seq = max(seq_rows // self.tm, 1)
        self.n_tiles = self.rows // self.tm
        self.tail_rows = self.tm if self.short else SUBLANES

    def row_spec(self, c):
        return pl.BlockSpec((self.tm, c), lambda i: (i, 0))

    def tail_spec(self, c):
        return pl.BlockSpec((self.tail_rows, c), lambda i: (i, 0))

    def last_rows(self, tail, n):
        c = tail.shape[-1]
        if self.short:
            return tail.reshape(self.n_seq, self.seq_rows, c)[:, self.seq_rows - n:]
        t = tail.reshape(self.n_seq, self.tiles_per_seq, SUBLANES, c)
        return t[:, -1, SUBLANES - n:]

    def fix_rows(self, state, k):
        n_prev, c = state.shape[1], state.shape[2]
        pad = jnp.zeros((self.n_seq, self.seq_rows - k, c), state.dtype)
        return jnp.concatenate([state[:, n_prev - k:], pad], axis=1).reshape(self.rows, c)


def _whole(a):
    nd = a.ndim
    return pl.BlockSpec(a.shape, lambda *_: (0,) * nd, pipeline_mode=pl.Buffered(1))


def _params(*sem):
    return pltpu.CompilerParams(dimension_semantics=sem, vmem_limit_bytes=VMEM_LIMIT)


def _row2(v):
    return v.reshape(1, -1).astype(F32)


def _rwkv_proj(grp, x, shift, ng, mu, w0, a0, k_k, k_a, weights):
    d = x.shape[1]
    vecs = [_row2(ng), mu.astype(F32), _row2(w0), _row2(a0), _row2(k_k), _row2(k_a)]
    ins, specs = [x], [grp.row_spec(d)]
    if grp.short:
        ins.append(grp.fix_rows(shift[:, None, :], 1))
        specs.append(grp.row_spec(d))
    consts = vecs + list(weights)
    out = jax.ShapeDtypeStruct((grp.rows, d), F32)
    return pl.pallas_call(
        functools.partial(_rwkv_proj_kernel, tiles_per_seq=grp.tiles_per_seq,
                          seq_rows=grp.seq_rows),
        grid=(grp.n_tiles,),
        in_specs=specs + [_whole(a) for a in consts],
        out_specs=[grp.row_spec(d)] * 8,
        out_shape=[out] * 8,
        scratch_shapes=[pltpu.VMEM((grp.tm + 2 * SUBLANES, d), F32)],
        compiler_params=_params("arbitrary"),
        name="rwkv_proj",
    )(*ins, *consts)


def _wkv(n_seq, seq_rows, chunk, r, k, v, lw, kk, b, s0):
    d = r.shape[1]
    n_pairs = d // LANES
    step = WKV_STEP if seq_rows % WKV_STEP == 0 and WKV_STEP % chunk == 0 else chunk
    n_steps = seq_rows // step
    tok = pl.BlockSpec((step, d), lambda bi, c: (bi * n_steps + c, 0))
    st = pl.BlockSpec((None, d // HEAD, HEAD, HEAD), lambda bi, c: (bi, 0, 0, 0))
    return pl.pallas_call(
        functools.partial(_wkv_kernel, chunk=chunk),
        grid=(n_seq, n_steps),
        in_specs=[tok] * 6 + [st],
        out_specs=[tok, st],
        out_shape=[jax.ShapeDtypeStruct(r.shape, F32), jax.ShapeDtypeStruct(s0.shape, F32)],
        scratch_shapes=[pltpu.VMEM((n_pairs, LANES, LANES), F32)],
        compiler_params=_params("arbitrary", "arbitrary"),
        name="wkv",
    )(r, k, v, lw, kk, b, s0)


def _rwkv_out(grp, o, r, k, v, g, x, lng, lnb, rk, wo):
    d = x.shape[1]
    consts = [_row2(lng), _row2(lnb), _row2(rk), wo]
    return pl.pallas_call(
        _rwkv_out_kernel,
        grid=(grp.n_tiles,),
        in_specs=[grp.row_spec(d)] * 6 + [_whole(a) for a in consts],
        out_specs=grp.row_spec(d),
        out_shape=jax.ShapeDtypeStruct(x.shape, F32),
        compiler_params=_params("arbitrary"),
        name="rwkv_out",
    )(o, r, k, v, g, x, *consts)


def _fox_proj(grp, x, ng, wq, wk, wv, wg, wf, fb, qg, kg):
    d = x.shape[1]
    consts = [_row2(ng), wq, wk, wv, wg, wf, fb, qg, kg]
    wide = jax.ShapeDtypeStruct((grp.rows, d), F32)
    thin = jax.ShapeDtypeStruct((grp.rows, LANES), F32)
    out_specs = [grp.row_spec(d)] * 4 + [grp.row_spec(LANES)] * 2
    out_shape = [wide] * 4 + [thin] * 2
    if not grp.short:
        tps = grp.tiles_per_seq
        out_specs += [pl.BlockSpec((None, d, grp.tm), lambda i: (i // tps, 0, i % tps))] * 2
        out_shape += [jax.ShapeDtypeStruct((grp.n_seq, d, grp.seq_rows), F32)] * 2
    return pl.pallas_call(
        functools.partial(_fox_proj_kernel, tiles_per_seq=grp.tiles_per_seq),
        grid=(grp.n_tiles,),
        in_specs=[grp.row_spec(d)] + [_whole(a) for a in consts],
        out_specs=out_specs,
        out_shape=out_shape,
        scratch_shapes=[pltpu.VMEM((SUBLANES, LANES), F32)],
        compiler_params=_params("arbitrary"),
        name="fox_proj",
    )(x, *consts)


def _fox_attn(n_seq, seq_rows, tq, q, k, v, fcum):
    d = q.shape[1]
    n_pairs = d // LANES
    nq = seq_rows // tq
    h = d // HEAD
    f = fcum[:, :h].reshape(n_seq, nq, tq, n_pairs, 2).transpose(0, 3, 1, 2, 4)
    kv = pl.BlockSpec((None, nq, tq, LANES), lambda b, p, i: (b, 0, 0, p))
    qo = pl.BlockSpec((tq, LANES), lambda b, p, i: (b * nq + i, p))
    return pl.pallas_call(
        _fox_attn_kernel,
        grid=(n_seq, n_pairs, nq),
        in_specs=[qo, kv, kv,
                  pl.BlockSpec((None, None, nq, tq, 2), lambda b, p, i: (b, p, 0, 0, 0))],
        out_specs=qo,
        out_shape=jax.ShapeDtypeStruct(q.shape, F32),
        scratch_shapes=[pltpu.VMEM((2, nq, tq, LANES), BF16), pltpu.VMEM((nq, LANES, tq), BF16)],
        compiler_params=_params("arbitrary", "arbitrary", "arbitrary"),
        name="fox_attn",
    )(q, k.reshape(n_seq, nq, tq, d), v.reshape(n_seq, nq, tq, d), f)


def _fox_paged(n_seq, ts, q, k, v, lf, cache_k, cache_v, cache_lf, page_table):
    d = q.shape[1]
    n_heads = d // HEAD
    rows = n_heads * ts
    ps = cache_k.shape[1]
    n_pages = page_table.shape[1]
    ck = jnp.transpose(cache_k, (0, 2, 3, 1))
    cv = jnp.transpose(cache_v, (0, 2, 3, 1))
    clf = jnp.transpose(cache_lf.astype(F32), (0, 2, 1))

    n_sub = next(n for n in (PAGES_PER_STEP, 2, 1) if n_pages % n == 0)

    def slot(i):
        return lambda b, s, pt: pt[b * n_pages + n_pages - n_sub * jnp.maximum(s, 1) + i]

    tok = lambda c: pl.BlockSpec((ts, c), lambda b, s, pt: (b, 0))
    cache = [pl.BlockSpec((None, n_heads, HEAD, ps),
                          lambda b, s, pt, f=slot(i): (f(b, s, pt), 0, 0, 0)) for i in range(n_sub)]
    gates = [pl.BlockSpec((None, n_heads, ps),
                          lambda b, s, pt, f=slot(i): (f(b, s, pt), 0, 0)) for i in range(n_sub)]
    grid_spec = pltpu.PrefetchScalarGridSpec(
        num_scalar_prefetch=1,
        grid=(n_seq, n_pages // n_sub + 1),
        in_specs=[tok(d), tok(d), tok(d), tok(LANES)] + cache + cache + gates,
        out_specs=tok(d),
        scratch_shapes=[pltpu.VMEM((n_heads, ts, HEAD), F32), pltpu.VMEM((rows, 1), F32),
                        pltpu.VMEM((rows, 1), F32), pltpu.VMEM((rows, HEAD), F32),
                        pltpu.VMEM((n_heads, 1), F32)],
    )
    return pl.pallas_call(
        functools.partial(_fox_paged_kernel, n_sub=n_sub),
        grid_spec=grid_spec,
        out_shape=jax.ShapeDtypeStruct(q.shape, F32),
        compiler_params=_params("arbitrary", "arbitrary"),
        name="fox_paged",
    )(page_table.reshape(-1).astype(jnp.int32), q, k, v, lf,
      *([ck] * n_sub), *([cv] * n_sub), *([clf] * n_sub))


def _fox_out(grp, o, g, x, wo):
    d = x.shape[1]
    return pl.pallas_call(
        _fox_out_kernel,
        grid=(grp.n_tiles,),
        in_specs=[grp.row_spec(d)] * 3 + [_whole(wo)],
        out_specs=grp.row_spec(d),
        out_shape=jax.ShapeDtypeStruct(x.shape, F32),
        compiler_params=_params("arbitrary"),
        name="fox_out",
    )(o, g, x, wo)


def _conv_mix(grp, x, state, ng, win, cw, wo):
    d = x.shape[1]
    ins, specs = [x], [grp.row_spec(d)]
    if grp.short:
        ins += [grp.fix_rows(state, 1), grp.fix_rows(state, 2)]
        specs += [grp.row_spec(d)] * 2
    consts = [_row2(ng), win, cw.astype(F32), wo]
    return pl.pallas_call(
        functools.partial(_conv_mix_kernel, tiles_per_seq=grp.tiles_per_seq,
                          seq_rows=grp.seq_rows),
        grid=(grp.n_tiles,),
        in_specs=specs + [_whole(a) for a in consts],
        out_specs=[grp.row_spec(d), grp.tail_spec(d)],
        out_shape=[jax.ShapeDtypeStruct(x.shape, F32),
                   jax.ShapeDtypeStruct((grp.n_tiles * grp.tail_rows, d), F32)],
        scratch_shapes=[pltpu.VMEM((grp.tm + 2 * SUBLANES, d), F32)],
        compiler_params=_params("arbitrary"),
        name="conv_mix",
    )(*ins, *consts)


def _ffn(grp, x, state, ng, wup, cw, cb, wdn):
    d = x.shape[1]
    dff = cw.shape[1]
    n_chunks = 2 if dff % (2 * LANES) == 0 else 1
    ins, specs = [x], [grp.row_spec(d)]
    if grp.short:
        ins += [grp.fix_rows(state, 1), grp.fix_rows(state, 2)]
        specs += [grp.row_spec(dff)] * 2
    consts = [_row2(ng), wup, cw.astype(F32), _row2(cb), wdn]
    return pl.pallas_call(
        functools.partial(_ffn_kernel, tiles_per_seq=grp.tiles_per_seq, seq_rows=grp.seq_rows,
                          n_chunks=n_chunks),
        grid=(grp.n_tiles,),
        in_specs=specs + [_whole(a) for a in consts],
        out_specs=[grp.row_spec(d), grp.tail_spec(dff)],
        out_shape=[jax.ShapeDtypeStruct(x.shape, F32),
                   jax.ShapeDtypeStruct((grp.n_tiles * grp.tail_rows, dff), F32)],
        scratch_shapes=[pltpu.VMEM((grp.tm + 2 * SUBLANES, dff), F32)],
        compiler_params=_params("arbitrary"),
        name="conv_ffn",
    )(*ins, *consts)


def kernel(x_prompt, x_sample, state_wkv, state_shift, cache_k, cache_v, cache_logf, state_conv_mix, state_conv_ffn, page_table, norm_mix_g, norm_ffn_g, a_mu, a_w_rkv, a_w0, a_w1, a_w2, a_a0, a_a1, a_a2, a_g1, a_g2, a_k_k, a_k_a, a_r_k, a_lnx_g, a_lnx_b, a_w_out, b_w_in, b_f_bias, b_q_norm_g, b_k_norm_g, b_w_out, c_w_in, c_conv_w, c_w_out, f_w_up, f_conv_w, f_conv_b, f_w_down):
    bp, tp, d = x_prompt.shape
    db, ts, _ = x_sample.shape
    depth = norm_mix_g.shape[0]
    n_heads = d // HEAD
    assert d % LANES == 0 and c_conv_w.shape[1] == CONV_TAPS and f_conv_w.shape[1] == CONV_TAPS
    gp = _Group(bp, tp, ROW_TILE)
    gs = _Group(db, ts, ROW_TILE)
    groups = (gp, gs)
    xs = [x_prompt.reshape(gp.rows, d), x_sample.reshape(gs.rows, d)]
    bf = lambda w: w.astype(BF16)
    chunk_p = WKV_CHUNK if tp % WKV_CHUNK == 0 else tp
    assert chunk_p <= WKV_CHUNK

    wkv_o, shift_o = ([], []), ([], [])
    k_o, v_o, lf_o = ([], []), ([], []), ([], [])
    cm_o, cf_o = ([], []), ([], [])
    for i in range(depth):
        kind, j = i % 3, i // 3
        if kind == 0:
            weights = [bf(a_w_rkv[j, 0]), bf(a_w_rkv[j, 1]), bf(a_w_rkv[j, 2]), bf(a_w1[j]),
                       bf(a_w2[j]), bf(a_a1[j]), bf(a_a2[j]), bf(a_g1[j]), bf(a_g2[j])]
            wo = bf(a_w_out[j])
            for gi, grp in enumerate(groups):
                shift = state_shift[j] if gi == 1 else None
                r, k, v, lw, kk, b, g, xn = _rwkv_proj(
                    grp, xs[gi], shift, norm_mix_g[i], a_mu[j], a_w0[j], a_a0[j], a_k_k[j],
                    a_k_a[j], weights)
                if gi == 0:
                    s0 = jnp.zeros((grp.n_seq, n_heads, HEAD, HEAD), F32)
                    chunk = chunk_p
                else:
                    s0 = state_wkv[j].astype(F32)
                    chunk = ts
                o, s_new = _wkv(grp.n_seq, grp.seq_rows, chunk, r, k, v, lw, kk, b, s0)
                xs[gi] = _rwkv_out(grp, o, r, k, v, g, xs[gi], a_lnx_g[j], a_lnx_b[j],
                                   a_r_k[j].reshape(-1), wo)
                wkv_o[gi].append(s_new.astype(state_wkv.dtype if gi else F32))
                shift_o[gi].append(xn.reshape(grp.n_seq, grp.seq_rows, d)[:, -1])
        elif kind == 1:
            w_in = b_w_in[j]
            wq, wk, wv, wg = (bf(w_in[:, c * d:(c + 1) * d]) for c in range(4))
            wf = bf(jnp.pad(w_in[:, 4 * d:], ((0, 0), (0, LANES - n_heads))))
            fb = jnp.pad(b_f_bias[j].astype(F32), (0, LANES - n_heads)).reshape(1, LANES)
            qg = jnp.tile(b_q_norm_g[j].astype(F32), n_heads).reshape(1, d)
            kg = jnp.tile(b_k_norm_g[j].astype(F32), n_heads).reshape(1, d)
            wo = bf(b_w_out[j])
            for gi, grp in enumerate(groups):
                q, k, v, g, lf, fc, *kv_t = _fox_proj(grp, xs[gi], norm_mix_g[i], wq, wk, wv, wg,
                                                      wf, fb, qg, kg)
                if gi == 0:
                    tile = ATTN_TILE if grp.seq_rows % ATTN_TILE == 0 else grp.tm
                    o = _fox_attn(grp.n_seq, grp.seq_rows, tile, q, k, v, fc)
                else:
                    o = _fox_paged(grp.n_seq, grp.seq_rows, q, k, v, lf, cache_k[j], cache_v[j],
                                   cache_logf[j], page_table)
                xs[gi] = _fox_out(grp, o, g, xs[gi], wo)
                shape = (grp.n_seq, grp.seq_rows, n_heads, HEAD)
                if kv_t:
                    rows_last = lambda t: jnp.transpose(
                        t.reshape(grp.n_seq, n_heads, HEAD, grp.seq_rows), (0, 3, 1, 2))
                    k_o[gi].append(rows_last(kv_t[0]))
                    v_o[gi].append(rows_last(kv_t[1]))
                else:
                    k_o[gi].append(k.reshape(shape))
                    v_o[gi].append(v.reshape(shape))
                lf_o[gi].append(lf[:, :n_heads].reshape(shape[:3]))
        else:
            win, wo = bf(c_w_in[j]), bf(c_w_out[j])
            for gi, grp in enumerate(groups):
                state = state_conv_mix[j] if gi == 1 else None
                xs[gi], tail = _conv_mix(grp, xs[gi], state, norm_mix_g[i], win, c_conv_w[j], wo)
                cm_o[gi].append(grp.last_rows(tail, CONV_TAPS - 1))
        wup, wdn = bf(f_w_up[i]), bf(f_w_down[i])
        for gi, grp in enumerate(groups):
            state = state_conv_ffn[i] if gi == 1 else None
            xs[gi], tail = _ffn(grp, xs[gi], state, norm_ffn_g[i], wup, f_conv_w[i],
                                f_conv_b[i], wdn)
            cf_o[gi].append(grp.last_rows(tail, CONV_TAPS - 1))

    st = jnp.stack
    return (xs[0].reshape(bp, tp, d), xs[1].reshape(db, ts, d),
            st(wkv_o[0]), st(shift_o[0]), st(wkv_o[1]), st(shift_o[1]),
            st(k_o[0]), st(v_o[0]), st(lf_o[0]), st(k_o[1]), st(v_o[1]), st(lf_o[1]),
            st(cm_o[0]), st(cm_o[1]), st(cf_o[0]), st(cf_o[1]))
```

```python
import functools

import jax
import jax.numpy as jnp
from jax import lax
from jax.experimental import pallas as pl
from jax.experimental.pallas import tpu as pltpu

HEAD = 64
LANES = 128
SUBLANES = 8
LOG2_E = 1.4426950408889634
RMS_EPS = 1e-6
GN_EPS = 64e-5
CONV_TAPS = 3
WKV_CHUNK = 64
WKV_STEP = 256
ROW_TILE = 512
ATTN_TILE = 512
PAGES_PER_STEP = 8
VMEM_LIMIT = 56 * 1024 * 1024

F32 = jnp.float32
BF16 = jnp.bfloat16
_HI = lax.Precision.HIGHEST


def _mm(x, y, precision=None):
    return jnp.dot(x, y, precision=precision, preferred_element_type=F32)


def _nt(x, y, precision=None):
    return lax.dot_general(x, y, (((1,), (1,)), ((), ())), precision=precision,
                           preferred_element_type=F32)


def _tn(x, y, precision=None):
    return lax.dot_general(x, y, (((0,), (0,)), ((), ())), precision=precision,
                           preferred_element_type=F32)


def _iota(shape, dim):
    return lax.broadcasted_iota(jnp.int32, shape, dim)


def _sigmoid(x):
    return 1.0 / (1.0 + jnp.exp(-x))


def _softplus(x):
    return jnp.maximum(x, 0.0) + jnp.log(1.0 + jnp.exp(-jnp.abs(x)))


def _rms(x, g):
    return x * lax.rsqrt(jnp.mean(x * x, axis=-1, keepdims=True) + RMS_EPS) * g


def _pair_ones():
    r = lax.div(_iota((LANES, LANES), 0), HEAD)
    c = lax.div(_iota((LANES, LANES), 1), HEAD)
    return (r == c).astype(BF16)


def _head_sums(x, ones):
    hi = x.astype(BF16)
    lo = (x - hi.astype(F32)).astype(BF16)
    parts = [_mm(hi[:, i:i + LANES], ones) + _mm(lo[:, i:i + LANES], ones)
             for i in range(0, x.shape[1], LANES)]
    return jnp.concatenate(parts, axis=1)


def _stage_rows(sh_ref, cols, u, first_tile):
    tm = u.shape[0]

    @pl.when(first_tile)
    def _():
        sh_ref[0:SUBLANES, cols] = jnp.zeros((SUBLANES, u.shape[1]), u.dtype)

    sh_ref[SUBLANES:SUBLANES + tm, cols] = u


def _prev_rows(sh_ref, cols, tm, k, pos, fix_ref):
    prev = sh_ref[SUBLANES - k:SUBLANES - k + tm, cols]
    if fix_ref is not None:
        prev = jnp.where(pos >= k, prev, fix_ref[:, cols])
    return prev


def _keep_tail(sh_ref, tm):
    sh_ref[0:SUBLANES, :] = sh_ref[tm:tm + SUBLANES, :]


def _seq_pos(tm, seq_rows):
    return lax.rem(_iota((tm, 1), 0), seq_rows)


def _rwkv_proj_kernel(*refs, tiles_per_seq, seq_rows):
    short = seq_rows < refs[0].shape[0]
    x_ref, refs = refs[0], refs[1:]
    fix_ref = None
    if short:
        fix_ref, refs = refs[0], refs[1:]
    (ng_ref, mu_ref, w0_ref, a0_ref, kk_ref, ka_ref,
     wr_ref, wk_ref, wv_ref, w1_ref, w2_ref, a1_ref, a2_ref, g1_ref, g2_ref,
     r_o, k_o, v_o, lw_o, kkn_o, b_o, g_o, xn_o, sh_ref) = refs
    tm = x_ref.shape[0]
    i = pl.program_id(0)
    xn = _rms(x_ref[...], ng_ref[...])
    xn_o[...] = xn
    cols = slice(None)
    _stage_rows(sh_ref, cols, xn, lax.rem(i, tiles_per_seq) == 0)
    pos = _seq_pos(tm, seq_rows) if short else None
    dx = _prev_rows(sh_ref, cols, tm, 1, pos, fix_ref) - xn
    _keep_tail(sh_ref, tm)

    def mix(c):
        return (xn + dx * mu_ref[c:c + 1, :]).astype(BF16)

    r = _mm(mix(0), wr_ref[...])
    k = _mm(mix(2), wk_ref[...])
    v = _mm(mix(3), wv_ref[...])
    wl = w0_ref[...] + _mm(jnp.tanh(_mm(mix(1), w1_ref[...])).astype(BF16), w2_ref[...])
    lw = -jnp.exp(-_softplus(-wl) - 0.5)
    iclr = _sigmoid(a0_ref[...] + _mm(_mm(mix(4), a1_ref[...]).astype(BF16), a2_ref[...]))
    g = _mm(_sigmoid(_mm(mix(5), g1_ref[...])).astype(BF16), g2_ref[...])
    kk = k * kk_ref[...]
    norm = jnp.sqrt(_head_sums(kk * kk, _pair_ones()))
    kk = kk / jnp.maximum(norm, 1e-12)
    r_o[...] = r
    k_o[...] = k * (1.0 + (iclr - 1.0) * ka_ref[...])
    v_o[...] = v
    lw_o[...] = lw
    kkn_o[...] = kk
    b_o[...] = kk * iclr
    g_o[...] = g


def _wkv_kernel(r_ref, k_ref, v_ref, lw_ref, kk_ref, b_ref, s0_ref, o_ref, so_ref, s_ref, *,
                chunk):
    c = pl.program_id(1)
    rows, d = r_ref.shape
    n_pairs = d // LANES

    @pl.when(c == 0)
    def _():
        zero = jnp.zeros((HEAD, HEAD), F32)
        for p in range(n_pairs):
            s_ref[p] = jnp.concatenate(
                [jnp.concatenate([s0_ref[2 * p], zero], axis=1),
                 jnp.concatenate([zero, s0_ref[2 * p + 1]], axis=1)], axis=0)

    tril = (_iota((chunk, chunk), 0) >= _iota((chunk, chunk), 1)).astype(F32)
    first = _iota((chunk, LANES), 1) < HEAD
    rows2 = 2 * chunk
    merged = rows2 % LANES == 0
    ri = lax.rem(_iota((rows2, rows2), 0), chunk)
    ci = lax.rem(_iota((rows2, rows2), 1), chunk)
    strict = ri > ci
    incl = ri >= ci
    eye = (_iota((rows2, rows2), 0) == _iota((rows2, rows2), 1)).astype(F32)
    lanes = [slice(p * LANES, (p + 1) * LANES) for p in range(n_pairs)]

    def stack(x):
        return jnp.concatenate([jnp.where(first, x, 0.0), jnp.where(first, 0.0, x)],
                               axis=0).astype(BF16)

    def scores(ar, bk):
        if merged:
            g = _nt(ar, bk)
            return g[:rows2, :rows2], g[:rows2, rows2:], g[rows2:, :rows2], g[rows2:, rows2:]
        a, r, b, k = ar[:rows2], ar[rows2:], bk[:rows2], bk[rows2:]
        return _nt(a, b), _nt(a, k), _nt(r, b), _nt(r, k)

    ar, bk, vx, sc, g_end, tok = [], [], [], [], [], []
    for t0 in range(0, rows, chunk):
        ts = slice(t0, t0 + chunk)
        lw = lw_ref[ts, :]
        cum = _mm(tril, lw, _HI)
        e_pos = jnp.exp(cum)
        e_neg = jnp.exp(-cum)
        a_t = -(kk_ref[ts, :] * jnp.exp(cum - lw))
        b_t = b_ref[ts, :] * e_neg
        k_t = k_ref[ts, :] * e_neg
        r_t = r_ref[ts, :] * e_pos
        v_t = v_ref[ts, :]
        for sl in lanes:
            ar.append(jnp.concatenate([stack(a_t[:, sl]), stack(r_t[:, sl])], axis=0))
            bk.append(jnp.concatenate([stack(b_t[:, sl]), stack(k_t[:, sl])], axis=0))
            vx.append(stack(v_t[:, sl]))
            sc.append(scores(ar[-1], bk[-1]))
            g_end.append(e_pos[chunk - 1:chunk, sl])
            tok.append(ts)
    items = range(len(ar))
    pw = [jnp.where(strict, sc[i][0], 0.0) for i in items]
    inv = [eye + pw[i] for i in items]
    for _ in range(max(chunk.bit_length() - 2, 0)):
        pwb = [pw[i].astype(BF16) for i in items]
        pw = [_mm(pwb[i], pwb[i]) for i in items]
        inv = [inv[i] + _mm(pw[i].astype(BF16), inv[i].astype(BF16)) for i in items]
    invb = [inv[i].astype(BF16) for i in items]
    w = [_mm(invb[i], ar[i][:rows2]).astype(BF16) for i in items]
    uv0 = [_mm(invb[i], _mm(jnp.where(strict, sc[i][1], 0.0).astype(BF16), vx[i]).astype(BF16))
           for i in items]
    for i in items:
        p = i % n_pairs
        m_rb = jnp.where(incl, sc[i][2], 0.0).astype(BF16)
        m_rk = jnp.where(incl, sc[i][3], 0.0).astype(BF16)
        s = s_ref[p]
        sb = s.astype(BF16)
        u = (_nt(w[i], sb) + uv0[i]).astype(BF16)
        uv = jnp.concatenate([u, vx[i]], axis=0)
        s_ref[p] = (s + _tn(uv, bk[i])) * g_end[i]
        if merged:
            oe = _nt(ar[i][rows2:], sb) + _mm(jnp.concatenate([m_rb, m_rk], axis=1), uv)
        else:
            oe = _nt(ar[i][rows2:], sb) + _mm(m_rb, u) + _mm(m_rk, vx[i])
        o_ref[tok[i], lanes[p]] = oe[:chunk] + oe[chunk:]

    @pl.when(c == pl.num_programs(1) - 1)
    def _():
        for p in range(n_pairs):
            s = s_ref[p]
            so_ref[2 * p] = s[:HEAD, :HEAD]
            so_ref[2 * p + 1] = s[HEAD:, HEAD:]


def _rwkv_out_kernel(o_ref, r_ref, k_ref, v_ref, g_ref, x_ref, lng_ref, lnb_ref, rk_ref,
                     wo_ref, y_ref):
    ones = _pair_ones()
    o = o_ref[...]
    d = o - _head_sums(o, ones) * (1.0 / HEAD)
    var = _head_sums(d * d, ones) * (1.0 / HEAD)
    on = d * lax.rsqrt(var + GN_EPS) * lng_ref[...] + lnb_ref[...]
    bonus = _head_sums(r_ref[...] * k_ref[...] * rk_ref[...], ones) * v_ref[...]
    z = ((on + bonus) * g_ref[...]).astype(BF16)
    y_ref[...] = x_ref[...] + _mm(z, wo_ref[...])


def _fox_proj_kernel(x_ref, ng_ref, wq_ref, wk_ref, wv_ref, wg_ref, wf_ref, fb_ref, qg_ref,
                     kg_ref, q_o, k_o, v_o, g_o, lf_o, fc_o, *rest, tiles_per_seq):
    tm = x_ref.shape[0]
    i = pl.program_id(0)
    xn = _rms(x_ref[...], ng_ref[...]).astype(BF16)
    ones = _pair_ones()
    q = _mm(xn, wq_ref[...])
    q_o[...] = q * lax.rsqrt(_head_sums(q * q, ones) * (1.0 / HEAD) + RMS_EPS) * qg_ref[...]
    k = _mm(xn, wk_ref[...])
    k = k * lax.rsqrt(_head_sums(k * k, ones) * (1.0 / HEAD) + RMS_EPS) * kg_ref[...]
    k_o[...] = k
    v = _mm(xn, wv_ref[...])
    v_o[...] = v
    if len(rest) == 3:
        kt_o, vt_o, carry_ref = rest
        kt_o[...] = k.T
        vt_o[...] = v.T
    else:
        (carry_ref,) = rest
    g_o[...] = _mm(xn, wg_ref[...])
    lf = -_softplus(-(_mm(xn, wf_ref[...]) + fb_ref[...]))
    lf_o[...] = lf

    @pl.when(lax.rem(i, tiles_per_seq) == 0)
    def _():
        carry_ref[...] = jnp.zeros(carry_ref.shape, F32)

    tril = (_iota((tm, tm), 0) >= _iota((tm, tm), 1)).astype(F32)
    fc = _mm(tril, lf, _HI) + carry_ref[0:1, :]
    fc_o[...] = fc
    carry_ref[...] = jnp.broadcast_to(fc[tm - 1:tm, :], carry_ref.shape)


def _fox_attn_kernel(q_ref, k_ref, v_ref, f_ref, o_ref, kaug_ref, vt_ref):
    qi = pl.program_id(2)
    n_blocks, tk, _ = k_ref.shape
    tq = q_ref.shape[0]
    lane = _iota((tk, LANES), 1)

    def augment(x, f, h, piece_off, const_off, const):
        rel = lane - HEAD * (1 - h)
        fh = f[:, h:h + 1] * LOG2_E
        hi = fh.astype(BF16).astype(F32)
        mid = (fh - hi).astype(BF16).astype(F32)
        lo = fh - hi - mid
        in_const = jnp.where(rel >= const_off, jnp.where(rel < const_off + 3, const, 0.0), 0.0)
        extra = jnp.where(rel == piece_off, hi,
                          jnp.where(rel == piece_off + 1, mid,
                                    jnp.where(rel == piece_off + 2, lo, in_const)))
        own = lax.div(lane, HEAD) == h
        return jnp.where(own, x, extra).astype(BF16)

    @pl.when(qi == 0)
    def _():
        def fill(j, carry):
            kb = k_ref[j]
            f = f_ref[j]
            for h in range(2):
                kaug_ref[h, j] = augment(kb, f, h, 0, 3, 1.0)
            vt_ref[j] = v_ref[j].T.astype(BF16)
            return carry
        lax.fori_loop(0, n_blocks, fill, 0)

    fq = f_ref[qi]
    q = q_ref[...] * (HEAD ** -0.5 * LOG2_E)
    qa = [augment(q, fq, h, 3, 0, -1.0) for h in range(2)]
    keep = _iota((tk, tq), 0) <= _iota((tk, tq), 1)

    def block(j, carry, diagonal):
        heads = range(2)
        m, l, acc = carry[0::3], carry[1::3], carry[2::3]
        vt = vt_ref[j]
        s = [_nt(kaug_ref[h, j], qa[h]) for h in heads]
        if diagonal:
            s = [jnp.where(keep, s[h], -jnp.inf) for h in heads]
        m_new = [jnp.maximum(m[h], jnp.max(s[h], axis=0, keepdims=True)) for h in heads]
        p = [jnp.exp2(s[h] - m_new[h]) for h in heads]
        alpha = [jnp.exp2(m[h] - m_new[h]) for h in heads]
        l = [alpha[h] * l[h] + jnp.sum(p[h], axis=0, keepdims=True) for h in heads]
        acc = [alpha[h] * acc[h] + _mm(vt, p[h].astype(BF16)) for h in heads]
        return (m_new[0], l[0], acc[0], m_new[1], l[1], acc[1])

    init = (jnp.full((1, tq), -jnp.inf, F32), jnp.zeros((1, tq), F32),
            jnp.zeros((LANES, tq), F32)) * 2
    carry = lax.fori_loop(0, qi, lambda j, c: block(j, c, False), init)
    m0, l0, acc0, m1, l1, acc1 = block(qi, carry, True)
    top = _iota((LANES, tq), 0) < HEAD
    o_ref[...] = jnp.where(top, acc0 / l0, acc1 / l1).T


def _fox_paged_kernel(pt_ref, q_ref, k_ref, v_ref, lf_ref, *refs, n_sub):
    del pt_ref
    ck_refs, cv_refs, clf_refs = refs[:n_sub], refs[n_sub:2 * n_sub], refs[2 * n_sub:3 * n_sub]
    o_ref, qh_ref, m_ref, l_ref, acc_ref, tot_ref = refs[3 * n_sub:]
    s_idx = pl.program_id(1)
    ts, d = q_ref.shape
    n_heads = d // HEAD
    rows = n_heads * ts

    def online(sc, pv):
        m = m_ref[...]
        m_new = jnp.maximum(m, jnp.max(sc, axis=1, keepdims=True))
        p = jnp.exp(sc - m_new)
        alpha = jnp.exp(m - m_new)
        l_ref[...] = alpha * l_ref[...] + jnp.sum(p, axis=1, keepdims=True)
        acc_ref[...] = alpha * acc_ref[...] + pv(p)
        m_ref[...] = m_new

    @pl.when(s_idx == 0)
    def _():
        q = q_ref[...] * (HEAD ** -0.5)
        for h in range(n_heads):
            qh_ref[h] = q[:, h * HEAD:(h + 1) * HEAD]
        m_ref[...] = jnp.full(m_ref.shape, -jnp.inf, F32)
        l_ref[...] = jnp.zeros(l_ref.shape, F32)
        acc_ref[...] = jnp.zeros(acc_ref.shape, F32)
        tot_ref[...] = jnp.zeros(tot_ref.shape, F32)
        qt = jnp.concatenate([q] * n_heads, axis=0)
        mine = lax.div(_iota((rows, d), 0), ts) == lax.div(_iota((rows, d), 1), HEAD)
        qx = jnp.where(mine, qt, 0.0).astype(BF16)
        lf = lf_ref[...]
        hsel = (lax.div(_iota((rows, lf.shape[1]), 0), ts)
                == _iota((rows, lf.shape[1]), 1)).astype(F32)
        lfx = _nt(hsel, lf, _HI)
        upto = (_iota((ts, ts), 0) <= _iota((ts, ts), 1)).astype(F32)
        cinc = _mm(lfx, upto, _HI)
        tq = lax.rem(_iota((rows, ts), 0), ts)
        sk = _iota((rows, ts), 1)
        sc = _nt(qx, k_ref[...].astype(BF16)) - cinc
        vb = v_ref[...].astype(BF16)

        def pv(p):
            full = _mm(p.astype(BF16), vb)
            return jnp.concatenate(
                [full[h * ts:(h + 1) * ts, h * HEAD:(h + 1) * HEAD] for h in range(n_heads)],
                axis=0)

        online(jnp.where(sk <= tq, sc, -jnp.inf), pv)

    @pl.when(s_idx > 0)
    def _():
        ps = ck_refs[0].shape[2]
        n_keys = n_sub * ps
        heads = (((2,), (1,)), ((0,), (0,)))
        keys = (((2,), (2,)), ((0,), (0,)))
        qh = qh_ref[...].astype(BF16)
        s3 = jnp.concatenate(
            [lax.dot_general(qh, r[...].astype(BF16), heads, preferred_element_type=F32)
             for r in ck_refs], axis=2)
        lf = jnp.concatenate([r[...] for r in clf_refs], axis=0)
        after = (_iota((ps, ps), 0) > _iota((ps, ps), 1)).astype(F32)
        inside = _mm(lf, after, _HI)
        page_sum = jnp.sum(lf, axis=1, keepdims=True)
        tot = tot_ref[...]
        bias = []
        for i in reversed(range(n_sub)):
            bias.append(inside[i * n_heads:(i + 1) * n_heads] + tot)
            tot = tot + page_sum[i * n_heads:(i + 1) * n_heads]
        bias = jnp.concatenate(bias[::-1], axis=1)
        sc = (s3 + bias[:, None, :]).reshape(rows, n_keys)

        def pv(p):
            p3 = p.reshape(n_heads, ts, n_keys).astype(BF16)
            o3 = sum(lax.dot_general(p3[:, :, i * ps:(i + 1) * ps], r[...].astype(BF16), keys,
                                     preferred_element_type=F32) for i, r in enumerate(cv_refs))
            return o3.reshape(rows, HEAD)

        online(sc, pv)
        tot_ref[...] = tot

    @pl.when(s_idx == pl.num_programs(1) - 1)
    def _():
        o = acc_ref[...] / l_ref[...]
        o_ref[...] = jnp.concatenate([o[h * ts:(h + 1) * ts, :] for h in range(n_heads)], axis=1)


def _fox_out_kernel(o_ref, g_ref, x_ref, wo_ref, y_ref):
    z = (o_ref[...] * _sigmoid(g_ref[...])).astype(BF16)
    y_ref[...] = x_ref[...] + _mm(z, wo_ref[...])


def _conv_mix_kernel(*refs, tiles_per_seq, seq_rows):
    short = seq_rows < refs[0].shape[0]
    x_ref, refs = refs[0], refs[1:]
    f1_ref = f2_ref = None
    if short:
        f1_ref, f2_ref, refs = refs[0], refs[1], refs[2:]
    ng_ref, win_ref, cw_ref, wo_ref, y_ref, tail_ref, sh_ref = refs
    tm, d = x_ref.shape
    i = pl.program_id(0)
    x = x_ref[...]
    xn = _rms(x, ng_ref[...]).astype(BF16)
    gb = _mm(xn, win_ref[:, 0:d])
    u = _mm(xn, win_ref[:, d:2 * d]) * _mm(xn, win_ref[:, 2 * d:3 * d])
    cols = slice(None)
    _stage_rows(sh_ref, cols, u, lax.rem(i, tiles_per_seq) == 0)
    pos = _seq_pos(tm, seq_rows) if short else None
    z = (cw_ref[0:1, :] * _prev_rows(sh_ref, cols, tm, 2, pos, f2_ref)
         + cw_ref[1:2, :] * _prev_rows(sh_ref, cols, tm, 1, pos, f1_ref)
         + cw_ref[2:3, :] * u)
    y_ref[...] = x + _mm((gb * z).astype(BF16), wo_ref[...])
    tr = tail_ref.shape[0]
    tail_ref[...] = sh_ref[SUBLANES + tm - tr:SUBLANES + tm, :]
    _keep_tail(sh_ref, tm)


def _ffn_kernel(*refs, tiles_per_seq, seq_rows):
    short = seq_rows < refs[0].shape[0]
    x_ref, refs = refs[0], refs[1:]
    f1_ref = f2_ref = None
    if short:
        f1_ref, f2_ref, refs = refs[0], refs[1], refs[2:]
    ng_ref, wup_ref, cw_ref, cb_ref, wdn_ref, y_ref, tail_ref, sh_ref = refs
    tm = x_ref.shape[0]
    dff = cw_ref.shape[1]
    i = pl.program_id(0)
    x = x_ref[...]
    xn = _rms(x, ng_ref[...]).astype(BF16)
    pos = _seq_pos(tm, seq_rows) if short else None
    cols = slice(None)
    gate = _mm(xn, wup_ref[:, 0:dff])
    val = _mm(xn, wup_ref[:, dff:2 * dff])
    _stage_rows(sh_ref, cols, gate, lax.rem(i, tiles_per_seq) == 0)
    conv = (cw_ref[0:1, :] * _prev_rows(sh_ref, cols, tm, 2, pos, f2_ref)
            + cw_ref[1:2, :] * _prev_rows(sh_ref, cols, tm, 1, pos, f1_ref)
            + cw_ref[2:3, :] * gate + cb_ref[...])
    h = conv * _sigmoid(conv) * val
    y_ref[...] = x + _mm(h.astype(BF16), wdn_ref[...])
    tr = tail_ref.shape[0]
    tail_ref[...] = sh_ref[SUBLANES + tm - tr:SUBLANES + tm, :]
    _keep_tail(sh_ref, tm)


class _Group:
    def __init__(self, n_seq, seq_rows, max_tile):
        self.n_seq, self.seq_rows = n_seq, seq_rows
        self.rows = n_seq * seq_rows
        if seq_rows % max_tile == 0:
            self.tm = max_tile
        else:
            self.tm = self.rows
            assert seq_rows % SUBLANES == 0 and seq_rows >= CONV_TAPS - 1
        self.short = seq_rows < self.tm
        self.tiles_per_seq = max(seq_rows // self.tm, 1)
        self.n_tiles = self.rows // self.tm
        self.tail_rows = self.tm if self.short else SUBLANES

    def row_spec(self, c):
        return pl.BlockSpec((self.tm, c), lambda i: (i, 0))

    def tail_spec(self, c):
        return pl.BlockSpec((self.tail_rows, c), lambda i: (i, 0))

    def last_rows(self, tail, n):
        c = tail.shape[-1]
        if self.short:
            return tail.reshape(self.n_seq, self.seq_rows, c)[:, self.seq_rows - n:]
        t = tail.reshape(self.n_seq, self.tiles_per_seq, SUBLANES, c)
        return t[:, -1, SUBLANES - n:]

    def fix_rows(self, state, k):
        n_prev, c = state.shape[1], state.shape[2]
        pad = jnp.zeros((self.n_seq, self.seq_rows - k, c), state.dtype)
        return jnp.concatenate([state[:, n_prev - k:], pad], axis=1).reshape(self.rows, c)


def _whole(a):
    nd = a.ndim
    return pl.BlockSpec(a.shape, lambda *_: (0,) * nd, pipeline_mode=pl.Buffered(1))


def _params(*sem):
    return pltpu.CompilerParams(dimension_semantics=sem, vmem_limit_bytes=VMEM_LIMIT)


def _row2(v):
    return v.reshape(1, -1).astype(F32)


def _rwkv_proj(grp, x, shift, ng, mu, w0, a0, k_k, k_a, weights):
    d = x.shape[1]
    vecs = [_row2(ng), mu.astype(F32), _row2(w0), _row2(a0), _row2(k_k), _row2(k_a)]
    ins, specs = [x], [grp.row_spec(d)]
    if grp.short:
        ins.append(grp.fix_rows(shift[:, None, :], 1))
        specs.append(grp.row_spec(d))
    consts = vecs + list(weights)
    out = jax.ShapeDtypeStruct((grp.rows, d), F32)
    return pl.pallas_call(
        functools.partial(_rwkv_proj_kernel, tiles_per_seq=grp.tiles_per_seq,
                          seq_rows=grp.seq_rows),
        grid=(grp.n_tiles,),
        in_specs=specs + [_whole(a) for a in consts],
        out_specs=[grp.row_spec(d)] * 8,
        out_shape=[out] * 8,
        scratch_shapes=[pltpu.VMEM((grp.tm + 2 * SUBLANES, d), F32)],
        compiler_params=_params("arbitrary"),
        name="rwkv_proj",
    )(*ins, *consts)


def _wkv(n_seq, seq_rows, chunk, r, k, v, lw, kk, b, s0):
    d = r.shape[1]
    n_pairs = d // LANES
    step = WKV_STEP if seq_rows % WKV_STEP == 0 and WKV_STEP % chunk == 0 else chunk
    n_steps = seq_rows // step
    tok = pl.BlockSpec((step, d), lambda bi, c: (bi * n_steps + c, 0))
    st = pl.BlockSpec((None, d // HEAD, HEAD, HEAD), lambda bi, c: (bi, 0, 0, 0))
    return pl.pallas_call(
        functools.partial(_wkv_kernel, chunk=chunk),
        grid=(n_seq, n_steps),
        in_specs=[tok] * 6 + [st],
        out_specs=[tok, st],
        out_shape=[jax.ShapeDtypeStruct(r.shape, F32), jax.ShapeDtypeStruct(s0.shape, F32)],
        scratch_shapes=[pltpu.VMEM((n_pairs, LANES, LANES), F32)],
        compiler_params=_params("arbitrary", "arbitrary"),
        name="wkv",
    )(r, k, v, lw, kk, b, s0)


def _rwkv_out(grp, o, r, k, v, g, x, lng, lnb, rk, wo):
    d = x.shape[1]
    consts = [_row2(lng), _row2(lnb), _row2(rk), wo]
    return pl.pallas_call(
        _rwkv_out_kernel,
        grid=(grp.n_tiles,),
        in_specs=[grp.row_spec(d)] * 6 + [_whole(a) for a in consts],
        out_specs=grp.row_spec(d),
        out_shape=jax.ShapeDtypeStruct(x.shape, F32),
        compiler_params=_params("arbitrary"),
        name="rwkv_out",
    )(o, r, k, v, g, x, *consts)


def _fox_proj(grp, x, ng, wq, wk, wv, wg, wf, fb, qg, kg):
    d = x.shape[1]
    consts = [_row2(ng), wq, wk, wv, wg, wf, fb, qg, kg]
    wide = jax.ShapeDtypeStruct((grp.rows, d), F32)
    thin = jax.ShapeDtypeStruct((grp.rows, LANES), F32)
    out_specs = [grp.row_spec(d)] * 4 + [grp.row_spec(LANES)] * 2
    out_shape = [wide] * 4 + [thin] * 2
    if not grp.short:
        tps = grp.tiles_per_seq
        out_specs += [pl.BlockSpec((None, d, grp.tm), lambda i: (i // tps, 0, i % tps))] * 2
        out_shape += [jax.ShapeDtypeStruct((grp.n_seq, d, grp.seq_rows), F32)] * 2
    return pl.pallas_call(
        functools.partial(_fox_proj_kernel, tiles_per_seq=grp.tiles_per_seq),
        grid=(grp.n_tiles,),
        in_specs=[grp.row_spec(d)] + [_whole(a) for a in consts],
        out_specs=out_specs,
        out_shape=out_shape,
        scratch_shapes=[pltpu.VMEM((SUBLANES, LANES), F32)],
        compiler_params=_params("arbitrary"),
        name="fox_proj",
    )(x, *consts)


def _fox_attn(n_seq, seq_rows, tq, q, k, v, fcum):
    d = q.shape[1]
    n_pairs = d // LANES
    nq = seq_rows // tq
    h = d // HEAD
    f = fcum[:, :h].reshape(n_seq, nq, tq, n_pairs, 2).transpose(0, 3, 1, 2, 4)
    kv = pl.BlockSpec((None, nq, tq, LANES), lambda b, p, i: (b, 0, 0, p))
    qo = pl.BlockSpec((tq, LANES), lambda b, p, i: (b * nq + i, p))
    return pl.pallas_call(
        _fox_attn_kernel,
        grid=(n_seq, n_pairs, nq),
        in_specs=[qo, kv, kv,
                  pl.BlockSpec((None, None, nq, tq, 2), lambda b, p, i: (b, p, 0, 0, 0))],
        out_specs=qo,
        out_shape=jax.ShapeDtypeStruct(q.shape, F32),
        scratch_shapes=[pltpu.VMEM((2, nq, tq, LANES), BF16), pltpu.VMEM((nq, LANES, tq), BF16)],
        compiler_params=_params("arbitrary", "arbitrary", "arbitrary"),
        name="fox_attn",
    )(q, k.reshape(n_seq, nq, tq, d), v.reshape(n_seq, nq, tq, d), f)


def _fox_paged(n_seq, ts, q, k, v, lf, cache_k, cache_v, cache_lf, page_table):
    d = q.shape[1]
    n_heads = d // HEAD
    rows = n_heads * ts
    ps = cache_k.shape[1]
    n_pages = page_table.shape[1]
    ck = jnp.transpose(cache_k, (0, 2, 3, 1))
    cv = jnp.transpose(cache_v, (0, 2, 3, 1))
    clf = jnp.transpose(cache_lf.astype(F32), (0, 2, 1))

    n_sub = next(n for n in (PAGES_PER_STEP, 2, 1) if n_pages % n == 0)

    def slot(i):
        return lambda b, s, pt: pt[b * n_pages + n_pages - n_sub * jnp.maximum(s, 1) + i]

    tok = lambda c: pl.BlockSpec((ts, c), lambda b, s, pt: (b, 0))
    cache = [pl.BlockSpec((None, n_heads, HEAD, ps),
                          lambda b, s, pt, f=slot(i): (f(b, s, pt), 0, 0, 0)) for i in range(n_sub)]
    gates = [pl.BlockSpec((None, n_heads, ps),
                          lambda b, s, pt, f=slot(i): (f(b, s, pt), 0, 0)) for i in range(n_sub)]
    grid_spec = pltpu.PrefetchScalarGridSpec(
        num_scalar_prefetch=1,
        grid=(n_seq, n_pages // n_sub + 1),
        in_specs=[tok(d), tok(d), tok(d), tok(LANES)] + cache + cache + gates,
        out_specs=tok(d),
        scratch_shapes=[pltpu.VMEM((n_heads, ts, HEAD), F32), pltpu.VMEM((rows, 1), F32),
                        pltpu.VMEM((rows, 1), F32), pltpu.VMEM((rows, HEAD), F32),
                        pltpu.VMEM((n_heads, 1), F32)],
    )
    return pl.pallas_call(
        functools.partial(_fox_paged_kernel, n_sub=n_sub),
        grid_spec=grid_spec,
        out_shape=jax.ShapeDtypeStruct(q.shape, F32),
        compiler_params=_params("arbitrary", "arbitrary"),
        name="fox_paged",
    )(page_table.reshape(-1).astype(jnp.int32), q, k, v, lf,
      *([ck] * n_sub), *([cv] * n_sub), *([clf] * n_sub))


def _fox_out(grp, o, g, x, wo):
    d = x.shape[1]
    return pl.pallas_call(
        _fox_out_kernel,
        grid=(grp.n_tiles,),
        in_specs=[grp.row_spec(d)] * 3 + [_whole(wo)],
        out_specs=grp.row_spec(d),
        out_shape=jax.ShapeDtypeStruct(x.shape, F32),
        compiler_params=_params("arbitrary"),
        name="fox_out",
    )(o, g, x, wo)


def _conv_mix(grp, x, state, ng, win, cw, wo):
    d = x.shape[1]
    ins, specs = [x], [grp.row_spec(d)]
    if grp.short:
        ins += [grp.fix_rows(state, 1), grp.fix_rows(state, 2)]
        specs += [grp.row_spec(d)] * 2
    consts = [_row2(ng), win, cw.astype(F32), wo]
    return pl.pallas_call(
        functools.partial(_conv_mix_kernel, tiles_per_seq=grp.tiles_per_seq,
                          seq_rows=grp.seq_rows),
        grid=(grp.n_tiles,),
        in_specs=specs + [_whole(a) for a in consts],
        out_specs=[grp.row_spec(d), grp.tail_spec(d)],
        out_shape=[jax.ShapeDtypeStruct(x.shape, F32),
                   jax.ShapeDtypeStruct((grp.n_tiles * grp.tail_rows, d), F32)],
        scratch_shapes=[pltpu.VMEM((grp.tm + 2 * SUBLANES, d), F32)],
        compiler_params=_params("arbitrary"),
        name="conv_mix",
    )(*ins, *consts)


def _ffn(grp, x, state, ng, wup, cw, cb, wdn):
    d = x.shape[1]
    dff = cw.shape[1]
    ins, specs = [x], [grp.row_spec(d)]
    if grp.short:
        ins += [grp.fix_rows(state, 1), grp.fix_rows(state, 2)]
        specs += [grp.row_spec(dff)] * 2
    consts = [_row2(ng), wup, cw.astype(F32), _row2(cb), wdn]
    return pl.pallas_call(
        functools.partial(_ffn_kernel, tiles_per_seq=grp.tiles_per_seq, seq_rows=grp.seq_rows),
        grid=(grp.n_tiles,),
        in_specs=specs + [_whole(a) for a in consts],
        out_specs=[grp.row_spec(d), grp.tail_spec(dff)],
        out_shape=[jax.ShapeDtypeStruct(x.shape, F32),
                   jax.ShapeDtypeStruct((grp.n_tiles * grp.tail_rows, dff), F32)],
        scratch_shapes=[pltpu.VMEM((grp.tm + 2 * SUBLANES, dff), F32)],
        compiler_params=_params("arbitrary"),
        name="conv_ffn",
    )(*ins, *consts)


def kernel(x_prompt, x_sample, state_wkv, state_shift, cache_k, cache_v, cache_logf, state_conv_mix, state_conv_ffn, page_table, norm_mix_g, norm_ffn_g, a_mu, a_w_rkv, a_w0, a_w1, a_w2, a_a0, a_a1, a_a2, a_g1, a_g2, a_k_k, a_k_a, a_r_k, a_lnx_g, a_lnx_b, a_w_out, b_w_in, b_f_bias, b_q_norm_g, b_k_norm_g, b_w_out, c_w_in, c_conv_w, c_w_out, f_w_up, f_conv_w, f_conv_b, f_w_down):
    bp, tp, d = x_prompt.shape
    db, ts, _ = x_sample.shape
    depth = norm_mix_g.shape[0]
    n_heads = d // HEAD
    assert d % LANES == 0 and c_conv_w.shape[1] == CONV_TAPS and f_conv_w.shape[1] == CONV_TAPS
    gp = _Group(bp, tp, ROW_TILE)
    gs = _Group(db, ts, ROW_TILE)
    groups = (gp, gs)
    xs = [x_prompt.reshape(gp.rows, d), x_sample.reshape(gs.rows, d)]
    bf = lambda w: w.astype(BF16)
    chunk_p = WKV_CHUNK if tp % WKV_CHUNK == 0 else tp
    assert chunk_p <= WKV_CHUNK

    wkv_o, shift_o = ([], []), ([], [])
    k_o, v_o, lf_o = ([], []), ([], []), ([], [])
    cm_o, cf_o = ([], []), ([], [])
    for i in range(depth):
        kind, j = i % 3, i // 3
        if kind == 0:
            weights = [bf(a_w_rkv[j, 0]), bf(a_w_rkv[j, 1]), bf(a_w_rkv[j, 2]), bf(a_w1[j]),
                       bf(a_w2[j]), bf(a_a1[j]), bf(a_a2[j]), bf(a_g1[j]), bf(a_g2[j])]
            wo = bf(a_w_out[j])
            for gi, grp in enumerate(groups):
                shift = state_shift[j] if gi == 1 else None
                r, k, v, lw, kk, b, g, xn = _rwkv_proj(
                    grp, xs[gi], shift, norm_mix_g[i], a_mu[j], a_w0[j], a_a0[j], a_k_k[j],
                    a_k_a[j], weights)
                if gi == 0:
                    s0 = jnp.zeros((grp.n_seq, n_heads, HEAD, HEAD), F32)
                    chunk = chunk_p
                else:
                    s0 = state_wkv[j].astype(F32)
                    chunk = ts
                o, s_new = _wkv(grp.n_seq, grp.seq_rows, chunk, r, k, v, lw, kk, b, s0)
                xs[gi] = _rwkv_out(grp, o, r, k, v, g, xs[gi], a_lnx_g[j], a_lnx_b[j],
                                   a_r_k[j].reshape(-1), wo)
                wkv_o[gi].append(s_new.astype(state_wkv.dtype if gi else F32))
                shift_o[gi].append(xn.reshape(grp.n_seq, grp.seq_rows, d)[:, -1])
        elif kind == 1:
            w_in = b_w_in[j]
            wq, wk, wv, wg = (bf(w_in[:, c * d:(c + 1) * d]) for c in range(4))
            wf = bf(jnp.pad(w_in[:, 4 * d:], ((0, 0), (0, LANES - n_heads))))
            fb = jnp.pad(b_f_bias[j].astype(F32), (0, LANES - n_heads)).reshape(1, LANES)
            qg = jnp.tile(b_q_norm_g[j].astype(F32), n_heads).reshape(1, d)
            kg = jnp.tile(b_k_norm_g[j].astype(F32), n_heads).reshape(1, d)
            wo = bf(b_w_out[j])
            for gi, grp in enumerate(groups):
                q, k, v, g, lf, fc, *kv_t = _fox_proj(grp, xs[gi], norm_mix_g[i], wq, wk, wv, wg,
                                                      wf, fb, qg, kg)
                if gi == 0:
                    tile = ATTN_TILE if grp.seq_rows % ATTN_TILE == 0 else grp.tm
                    o = _fox_attn(grp.n_seq, grp.seq_rows, tile, q, k, v, fc)
                else:
                    o = _fox_paged(grp.n_seq, grp.seq_rows, q, k, v, lf, cache_k[j], cache_v[j],
                                   cache_logf[j], page_table)
                xs[gi] = _fox_out(grp, o, g, xs[gi], wo)
                shape = (grp.n_seq, grp.seq_rows, n_heads, HEAD)
                if kv_t:
                    rows_last = lambda t: jnp.transpose(
                        t.reshape(grp.n_seq, n_heads, HEAD, grp.seq_rows), (0, 3, 1, 2))
                    k_o[gi].append(rows_last(kv_t[0]))
                    v_o[gi].append(rows_last(kv_t[1]))
                else:
                    k_o[gi].append(k.reshape(shape))
                    v_o[gi].append(v.reshape(shape))
                lf_o[gi].append(lf[:, :n_heads].reshape(shape[:3]))
        else:
            win, wo = bf(c_w_in[j]), bf(c_w_out[j])
            for gi, grp in enumerate(groups):
                state = state_conv_mix[j] if gi == 1 else None
                xs[gi], tail = _conv_mix(grp, xs[gi], state, norm_mix_g[i], win, c_conv_w[j], wo)
                cm_o[gi].append(grp.last_rows(tail, CONV_TAPS - 1))
        wup, wdn = bf(f_w_up[i]), bf(f_w_down[i])
        for gi, grp in enumerate(groups):
            state = state_conv_ffn[i] if gi == 1 else None
            xs[gi], tail = _ffn(grp, xs[gi], state, norm_ffn_g[i], wup, f_conv_w[i],
                                f_conv_b[i], wdn)
            cf_o[gi].append(grp.last_rows(tail, CONV_TAPS - 1))

    st = jnp.stack
    return (xs[0].reshape(bp, tp, d), xs[1].reshape(db, ts, d),
            st(wkv_o[0]), st(shift_o[0]), st(wkv_o[1]), st(shift_o[1]),
            st(k_o[0]), st(v_o[0]), st(lf_o[0]), st(k_o[1]), st(v_o[1]), st(lf_o[1]),
            st(cm_o[0]), st(cm_o[1]), st(cf_o[0]), st(cf_o[1]))
```

```python
import functools

import jax
import jax.numpy as jnp
from jax import lax
from jax.experimental import pallas as pl
from jax.experimental.pallas import tpu as pltpu

HEAD = 64
LANES = 128
SUBLANES = 8
LOG2_E = 1.4426950408889634
RMS_EPS = 1e-6
GN_EPS = 64e-5
CONV_TAPS = 3
WKV_CHUNK = 64
WKV_STEP = 256
ROW_TILE = 512
ATTN_TILE = 1024
PAGES_PER_STEP = 8
VMEM_LIMIT = 56 * 1024 * 1024

F32 = jnp.float32
BF16 = jnp.bfloat16
_HI = lax.Precision.HIGHEST


def _mm(x, y, precision=None):
    return jnp.dot(x, y, precision=precision, preferred_element_type=F32)


def _nt(x, y, precision=None):
    return lax.dot_general(x, y, (((1,), (1,)), ((), ())), precision=precision,
                           preferred_element_type=F32)


def _tn(x, y, precision=None):
    return lax.dot_general(x, y, (((0,), (0,)), ((), ())), precision=precision,
                           preferred_element_type=F32)


def _iota(shape, dim):
    return lax.broadcasted_iota(jnp.int32, shape, dim)


def _sigmoid(x):
    return 1.0 / (1.0 + jnp.exp(-x))


def _softplus(x):
    return jnp.maximum(x, 0.0) + jnp.log(1.0 + jnp.exp(-jnp.abs(x)))


def _rms(x, g):
    return x * lax.rsqrt(jnp.mean(x * x, axis=-1, keepdims=True) + RMS_EPS) * g


def _pair_ones():
    r = lax.div(_iota((LANES, LANES), 0), HEAD)
    c = lax.div(_iota((LANES, LANES), 1), HEAD)
    return (r == c).astype(BF16)


def _head_sums(x, ones):
    hi = x.astype(BF16)
    lo = (x - hi.astype(F32)).astype(BF16)
    parts = [_mm(hi[:, i:i + LANES], ones) + _mm(lo[:, i:i + LANES], ones)
             for i in range(0, x.shape[1], LANES)]
    return jnp.concatenate(parts, axis=1)


def _stage_rows(sh_ref, cols, u, first_tile):
    tm = u.shape[0]

    @pl.when(first_tile)
    def _():
        sh_ref[0:SUBLANES, cols] = jnp.zeros((SUBLANES, u.shape[1]), u.dtype)

    sh_ref[SUBLANES:SUBLANES + tm, cols] = u


def _prev_rows(sh_ref, cols, tm, k, pos, fix_ref):
    prev = sh_ref[SUBLANES - k:SUBLANES - k + tm, cols]
    if fix_ref is not None:
        prev = jnp.where(pos >= k, prev, fix_ref[:, cols])
    return prev


def _keep_tail(sh_ref, tm):
    sh_ref[0:SUBLANES, :] = sh_ref[tm:tm + SUBLANES, :]


def _seq_pos(tm, seq_rows):
    return lax.rem(_iota((tm, 1), 0), seq_rows)


def _rwkv_proj_kernel(*refs, tiles_per_seq, seq_rows):
    short = seq_rows < refs[0].shape[0]
    x_ref, refs = refs[0], refs[1:]
    fix_ref = None
    if short:
        fix_ref, refs = refs[0], refs[1:]
    (ng_ref, mu_ref, w0_ref, a0_ref, kk_ref, ka_ref,
     wr_ref, wk_ref, wv_ref, w1_ref, w2_ref, a1_ref, a2_ref, g1_ref, g2_ref,
     r_o, k_o, v_o, lw_o, kkn_o, b_o, g_o, xn_o, sh_ref) = refs
    tm = x_ref.shape[0]
    i = pl.program_id(0)
    xn = _rms(x_ref[...], ng_ref[...])
    xn_o[...] = xn
    cols = slice(None)
    _stage_rows(sh_ref, cols, xn, lax.rem(i, tiles_per_seq) == 0)
    pos = _seq_pos(tm, seq_rows) if short else None
    dx = _prev_rows(sh_ref, cols, tm, 1, pos, fix_ref) - xn
    _keep_tail(sh_ref, tm)

    def mix(c):
        return (xn + dx * mu_ref[c:c + 1, :]).astype(BF16)

    r = _mm(mix(0), wr_ref[...])
    k = _mm(mix(2), wk_ref[...])
    v = _mm(mix(3), wv_ref[...])
    wl = w0_ref[...] + _mm(jnp.tanh(_mm(mix(1), w1_ref[...])).astype(BF16), w2_ref[...])
    lw = -jnp.exp(-_softplus(-wl) - 0.5)
    iclr = _sigmoid(a0_ref[...] + _mm(_mm(mix(4), a1_ref[...]).astype(BF16), a2_ref[...]))
    g = _mm(_sigmoid(_mm(mix(5), g1_ref[...])).astype(BF16), g2_ref[...])
    kk = k * kk_ref[...]
    norm = jnp.sqrt(_head_sums(kk * kk, _pair_ones()))
    kk = kk / jnp.maximum(norm, 1e-12)
    r_o[...] = r
    k_o[...] = k * (1.0 + (iclr - 1.0) * ka_ref[...])
    v_o[...] = v
    lw_o[...] = lw
    kkn_o[...] = kk
    b_o[...] = kk * iclr
    g_o[...] = g


def _wkv_kernel(r_ref, k_ref, v_ref, lw_ref, kk_ref, b_ref, s0_ref, o_ref, so_ref, s_ref, *,
                chunk):
    c = pl.program_id(1)
    rows, d = r_ref.shape
    n_pairs = d // LANES

    @pl.when(c == 0)
    def _():
        zero = jnp.zeros((HEAD, HEAD), F32)
        for p in range(n_pairs):
            s_ref[p] = jnp.concatenate(
                [jnp.concatenate([s0_ref[2 * p], zero], axis=1),
                 jnp.concatenate([zero, s0_ref[2 * p + 1]], axis=1)], axis=0)

    tril = (_iota((chunk, chunk), 0) >= _iota((chunk, chunk), 1)).astype(F32)
    first = _iota((chunk, LANES), 1) < HEAD
    rows2 = 2 * chunk
    merged = rows2 % LANES == 0
    ri = lax.rem(_iota((rows2, rows2), 0), chunk)
    ci = lax.rem(_iota((rows2, rows2), 1), chunk)
    strict = ri > ci
    incl = ri >= ci
    eye = (_iota((rows2, rows2), 0) == _iota((rows2, rows2), 1)).astype(F32)
    lanes = [slice(p * LANES, (p + 1) * LANES) for p in range(n_pairs)]

    def stack(x):
        return jnp.concatenate([jnp.where(first, x, 0.0), jnp.where(first, 0.0, x)],
                               axis=0).astype(BF16)

    def scores(ar, bk):
        if merged:
            g = _nt(ar, bk)
            return g[:rows2, :rows2], g[:rows2, rows2:], g[rows2:, :rows2], g[rows2:, rows2:]
        a, r, b, k = ar[:rows2], ar[rows2:], bk[:rows2], bk[rows2:]
        return _nt(a, b), _nt(a, k), _nt(r, b), _nt(r, k)

    ar, bk, vx, sc, g_end, tok = [], [], [], [], [], []
    for t0 in range(0, rows, chunk):
        ts = slice(t0, t0 + chunk)
        lw = lw_ref[ts, :]
        cum = _mm(tril, lw, _HI)
        e_pos = jnp.exp(cum)
        e_neg = jnp.exp(-cum)
        a_t = -(kk_ref[ts, :] * jnp.exp(cum - lw))
        b_t = b_ref[ts, :] * e_neg
        k_t = k_ref[ts, :] * e_neg
        r_t = r_ref[ts, :] * e_pos
        v_t = v_ref[ts, :]
        for sl in lanes:
            ar.append(jnp.concatenate([stack(a_t[:, sl]), stack(r_t[:, sl])], axis=0))
            bk.append(jnp.concatenate([stack(b_t[:, sl]), stack(k_t[:, sl])], axis=0))
            vx.append(stack(v_t[:, sl]))
            sc.append(scores(ar[-1], bk[-1]))
            g_end.append(e_pos[chunk - 1:chunk, sl])
            tok.append(ts)
    items = range(len(ar))
    pw = [jnp.where(strict, sc[i][0], 0.0) for i in items]
    inv = [eye + pw[i] for i in items]
    for _ in range(max(chunk.bit_length() - 2, 0)):
        pwb = [pw[i].astype(BF16) for i in items]
        pw = [_mm(pwb[i], pwb[i]) for i in items]
        inv = [inv[i] + _mm(pw[i].astype(BF16), inv[i].astype(BF16)) for i in items]
    invb = [inv[i].astype(BF16) for i in items]
    w = [_mm(invb[i], ar[i][:rows2]).astype(BF16) for i in items]
    uv0 = [_mm(invb[i], _mm(jnp.where(strict, sc[i][1], 0.0).astype(BF16), vx[i]).astype(BF16))
           for i in items]
    for i in items:
        p = i % n_pairs
        m_rb = jnp.where(incl, sc[i][2], 0.0).astype(BF16)
        m_rk = jnp.where(incl, sc[i][3], 0.0).astype(BF16)
        s = s_ref[p]
        sb = s.astype(BF16)
        u = (_nt(w[i], sb) + uv0[i]).astype(BF16)
        uv = jnp.concatenate([u, vx[i]], axis=0)
        s_ref[p] = (s + _tn(uv, bk[i])) * g_end[i]
        if merged:
            oe = _nt(ar[i][rows2:], sb) + _mm(jnp.concatenate([m_rb, m_rk], axis=1), uv)
        else:
            oe = _nt(ar[i][rows2:], sb) + _mm(m_rb, u) + _mm(m_rk, vx[i])
        o_ref[tok[i], lanes[p]] = oe[:chunk] + oe[chunk:]

    @pl.when(c == pl.num_programs(1) - 1)
    def _():
        for p in range(n_pairs):
            s = s_ref[p]
            so_ref[2 * p] = s[:HEAD, :HEAD]
            so_ref[2 * p + 1] = s[HEAD:, HEAD:]


def _rwkv_out_kernel(o_ref, r_ref, k_ref, v_ref, g_ref, x_ref, lng_ref, lnb_ref, rk_ref,
                     wo_ref, y_ref):
    ones = _pair_ones()
    o = o_ref[...]
    d = o - _head_sums(o, ones) * (1.0 / HEAD)
    var = _head_sums(d * d, ones) * (1.0 / HEAD)
    on = d * lax.rsqrt(var + GN_EPS) * lng_ref[...] + lnb_ref[...]
    bonus = _head_sums(r_ref[...] * k_ref[...] * rk_ref[...], ones) * v_ref[...]
    z = ((on + bonus) * g_ref[...]).astype(BF16)
    y_ref[...] = x_ref[...] + _mm(z, wo_ref[...])


def _fox_proj_kernel(x_ref, ng_ref, wq_ref, wk_ref, wv_ref, wg_ref, wf_ref, fb_ref, qg_ref,
                     kg_ref, q_o, k_o, v_o, g_o, lf_o, fc_o, *rest, tiles_per_seq):
    tm = x_ref.shape[0]
    i = pl.program_id(0)
    xn = _rms(x_ref[...], ng_ref[...]).astype(BF16)
    ones = _pair_ones()
    q = _mm(xn, wq_ref[...])
    q_o[...] = q * lax.rsqrt(_head_sums(q * q, ones) * (1.0 / HEAD) + RMS_EPS) * qg_ref[...]
    k = _mm(xn, wk_ref[...])
    k = k * lax.rsqrt(_head_sums(k * k, ones) * (1.0 / HEAD) + RMS_EPS) * kg_ref[...]
    k_o[...] = k
    v = _mm(xn, wv_ref[...])
    v_o[...] = v
    if len(rest) == 3:
        kt_o, vt_o, carry_ref = rest
        kt_o[...] = k.T
        vt_o[...] = v.T
    else:
        (carry_ref,) = rest
    g_o[...] = _mm(xn, wg_ref[...])
    lf = -_softplus(-(_mm(xn, wf_ref[...]) + fb_ref[...]))
    lf_o[...] = lf

    @pl.when(lax.rem(i, tiles_per_seq) == 0)
    def _():
        carry_ref[...] = jnp.zeros(carry_ref.shape, F32)

    tril = (_iota((tm, tm), 0) >= _iota((tm, tm), 1)).astype(F32)
    fc = _mm(tril, lf, _HI) + carry_ref[0:1, :]
    fc_o[...] = fc
    carry_ref[...] = jnp.broadcast_to(fc[tm - 1:tm, :], carry_ref.shape)


def _fox_attn_kernel(q_ref, k_ref, v_ref, f_ref, o_ref, kaug_ref, vt_ref):
    qi = pl.program_id(2)
    n_blocks, tk, _ = k_ref.shape
    tq = q_ref.shape[0]
    lane = _iota((tk, LANES), 1)

    def augment(x, f, h, piece_off, const_off, const):
        rel = lane - HEAD * (1 - h)
        fh = f[:, h:h + 1] * LOG2_E
        hi = fh.astype(BF16).astype(F32)
        mid = (fh - hi).astype(BF16).astype(F32)
        lo = fh - hi - mid
        in_const = jnp.where(rel >= const_off, jnp.where(rel < const_off + 3, const, 0.0), 0.0)
        extra = jnp.where(rel == piece_off, hi,
                          jnp.where(rel == piece_off + 1, mid,
                                    jnp.where(rel == piece_off + 2, lo, in_const)))
        own = lax.div(lane, HEAD) == h
        return jnp.where(own, x, extra).astype(BF16)

    @pl.when(qi == 0)
    def _():
        def fill(j, carry):
            kb = k_ref[j]
            f = f_ref[j]
            for h in range(2):
                kaug_ref[h, j] = augment(kb, f, h, 0, 3, 1.0)
            vt_ref[j] = v_ref[j].T.astype(BF16)
            return carry
        lax.fori_loop(0, n_blocks, fill, 0)

    fq = f_ref[qi]
    q = q_ref[...] * (HEAD ** -0.5 * LOG2_E)
    qa = [augment(q, fq, h, 3, 0, -1.0) for h in range(2)]
    keep = _iota((tk, tq), 0) <= _iota((tk, tq), 1)

    def block(j, carry, diagonal):
        heads = range(2)
        m, l, acc = carry[0::3], carry[1::3], carry[2::3]
        vt = vt_ref[j]
        s = [_nt(kaug_ref[h, j], qa[h]) for h in heads]
        if diagonal:
            s = [jnp.where(keep, s[h], -jnp.inf) for h in heads]
        m_new = [jnp.maximum(m[h], jnp.max(s[h], axis=0, keepdims=True)) for h in heads]
        p = [jnp.exp2(s[h] - m_new[h]) for h in heads]
        alpha = [jnp.exp2(m[h] - m_new[h]) for h in heads]
        l = [alpha[h] * l[h] + jnp.sum(p[h], axis=0, keepdims=True) for h in heads]
        acc = [alpha[h] * acc[h] + _mm(vt, p[h].astype(BF16)) for h in heads]
        return (m_new[0], l[0], acc[0], m_new[1], l[1], acc[1])

    init = (jnp.full((1, tq), -jnp.inf, F32), jnp.zeros((1, tq), F32),
            jnp.zeros((LANES, tq), F32)) * 2
    carry = lax.fori_loop(0, qi, lambda j, c: block(j, c, False), init)
    m0, l0, acc0, m1, l1, acc1 = block(qi, carry, True)
    top = _iota((LANES, tq), 0) < HEAD
    o_ref[...] = jnp.where(top, acc0 / l0, acc1 / l1).T


def _fox_paged_kernel(pt_ref, q_ref, k_ref, v_ref, lf_ref, *refs, n_sub):
    del pt_ref
    ck_refs, cv_refs, clf_refs = refs[:n_sub], refs[n_sub:2 * n_sub], refs[2 * n_sub:3 * n_sub]
    o_ref, qh_ref, m_ref, l_ref, acc_ref, tot_ref = refs[3 * n_sub:]
    s_idx = pl.program_id(1)
    ts, d = q_ref.shape
    n_heads = d // HEAD
    rows = n_heads * ts

    def online(sc, pv):
        m = m_ref[...]
        m_new = jnp.maximum(m, jnp.max(sc, axis=1, keepdims=True))
        p = jnp.exp(sc - m_new)
        alpha = jnp.exp(m - m_new)
        l_ref[...] = alpha * l_ref[...] + jnp.sum(p, axis=1, keepdims=True)
        acc_ref[...] = alpha * acc_ref[...] + pv(p)
        m_ref[...] = m_new

    @pl.when(s_idx == 0)
    def _():
        q = q_ref[...] * (HEAD ** -0.5)
        for h in range(n_heads):
            qh_ref[h] = q[:, h * HEAD:(h + 1) * HEAD]
        m_ref[...] = jnp.full(m_ref.shape, -jnp.inf, F32)
        l_ref[...] = jnp.zeros(l_ref.shape, F32)
        acc_ref[...] = jnp.zeros(acc_ref.shape, F32)
        tot_ref[...] = jnp.zeros(tot_ref.shape, F32)
        qt = jnp.concatenate([q] * n_heads, axis=0)
        mine = lax.div(_iota((rows, d), 0), ts) == lax.div(_iota((rows, d), 1), HEAD)
        qx = jnp.where(mine, qt, 0.0).astype(BF16)
        lf = lf_ref[...]
        hsel = (lax.div(_iota((rows, lf.shape[1]), 0), ts)
                == _iota((rows, lf.shape[1]), 1)).astype(F32)
        lfx = _nt(hsel, lf, _HI)
        upto = (_iota((ts, ts), 0) <= _iota((ts, ts), 1)).astype(F32)
        cinc = _mm(lfx, upto, _HI)
        tq = lax.rem(_iota((rows, ts), 0), ts)
        sk = _iota((rows, ts), 1)
        sc = _nt(qx, k_ref[...].astype(BF16)) - cinc
        vb = v_ref[...].astype(BF16)

        def pv(p):
            full = _mm(p.astype(BF16), vb)
            return jnp.concatenate(
                [full[h * ts:(h + 1) * ts, h * HEAD:(h + 1) * HEAD] for h in range(n_heads)],
                axis=0)

        online(jnp.where(sk <= tq, sc, -jnp.inf), pv)

    @pl.when(s_idx > 0)
    def _():
        ps = ck_refs[0].shape[2]
        n_keys = n_sub * ps
        heads = (((2,), (1,)), ((0,), (0,)))
        keys = (((2,), (2,)), ((0,), (0,)))
        qh = qh_ref[...].astype(BF16)
        s3 = jnp.concatenate(
            [lax.dot_general(qh, r[...].astype(BF16), heads, preferred_element_type=F32)
             for r in ck_refs], axis=2)
        lf = jnp.concatenate([r[...] for r in clf_refs], axis=0)
        after = (_iota((ps, ps), 0) > _iota((ps, ps), 1)).astype(F32)
        inside = _mm(lf, after, _HI)
        page_sum = jnp.sum(lf, axis=1, keepdims=True)
        tot = tot_ref[...]
        bias = []
        for i in reversed(range(n_sub)):
            bias.append(inside[i * n_heads:(i + 1) * n_heads] + tot)
            tot = tot + page_sum[i * n_heads:(i + 1) * n_heads]
        bias = jnp.concatenate(bias[::-1], axis=1)
        sc = (s3 + bias[:, None, :]).reshape(rows, n_keys)

        def pv(p):
            p3 = p.reshape(n_heads, ts, n_keys).astype(BF16)
            o3 = sum(lax.dot_general(p3[:, :, i * ps:(i + 1) * ps], r[...].astype(BF16), keys,
                                     preferred_element_type=F32) for i, r in enumerate(cv_refs))
            return o3.reshape(rows, HEAD)

        online(sc, pv)
        tot_ref[...] = tot

    @pl.when(s_idx == pl.num_programs(1) - 1)
    def _():
        o = acc_ref[...] / l_ref[...]
        o_ref[...] = jnp.concatenate([o[h * ts:(h + 1) * ts, :] for h in range(n_heads)], axis=1)


def _fox_out_kernel(o_ref, g_ref, x_ref, wo_ref, y_ref):
    z = (o_ref[...] * _sigmoid(g_ref[...])).astype(BF16)
    y_ref[...] = x_ref[...] + _mm(z, wo_ref[...])


def _conv_mix_kernel(*refs, tiles_per_seq, seq_rows):
    short = seq_rows < refs[0].shape[0]
    x_ref, refs = refs[0], refs[1:]
    f1_ref = f2_ref = None
    if short:
        f1_ref, f2_ref, refs = refs[0], refs[1], refs[2:]
    ng_ref, win_ref, cw_ref, wo_ref, y_ref, tail_ref, sh_ref = refs
    tm, d = x_ref.shape
    i = pl.program_id(0)
    x = x_ref[...]
    xn = _rms(x, ng_ref[...]).astype(BF16)
    gb = _mm(xn, win_ref[:, 0:d])
    u = _mm(xn, win_ref[:, d:2 * d]) * _mm(xn, win_ref[:, 2 * d:3 * d])
    cols = slice(None)
    _stage_rows(sh_ref, cols, u, lax.rem(i, tiles_per_seq) == 0)
    pos = _seq_pos(tm, seq_rows) if short else None
    z = (cw_ref[0:1, :] * _prev_rows(sh_ref, cols, tm, 2, pos, f2_ref)
         + cw_ref[1:2, :] * _prev_rows(sh_ref, cols, tm, 1, pos, f1_ref)
         + cw_ref[2:3, :] * u)
    y_ref[...] = x + _mm((gb * z).astype(BF16), wo_ref[...])
    tr = tail_ref.shape[0]
    tail_ref[...] = sh_ref[SUBLANES + tm - tr:SUBLANES + tm, :]
    _keep_tail(sh_ref, tm)


def _ffn_kernel(*refs, tiles_per_seq, seq_rows):
    short = seq_rows < refs[0].shape[0]
    x_ref, refs = refs[0], refs[1:]
    f1_ref = f2_ref = None
    if short:
        f1_ref, f2_ref, refs = refs[0], refs[1], refs[2:]
    ng_ref, wup_ref, cw_ref, cb_ref, wdn_ref, y_ref, tail_ref, sh_ref = refs
    tm = x_ref.shape[0]
    dff = cw_ref.shape[1]
    i = pl.program_id(0)
    x = x_ref[...]
    xn = _rms(x, ng_ref[...]).astype(BF16)
    pos = _seq_pos(tm, seq_rows) if short else None
    cols = slice(None)
    gate = _mm(xn, wup_ref[:, 0:dff])
    val = _mm(xn, wup_ref[:, dff:2 * dff])
    _stage_rows(sh_ref, cols, gate, lax.rem(i, tiles_per_seq) == 0)
    conv = (cw_ref[0:1, :] * _prev_rows(sh_ref, cols, tm, 2, pos, f2_ref)
            + cw_ref[1:2, :] * _prev_rows(sh_ref, cols, tm, 1, pos, f1_ref)
            + cw_ref[2:3, :] * gate + cb_ref[...])
    h = conv * _sigmoid(conv) * val
    y_ref[...] = x + _mm(h.astype(BF16), wdn_ref[...])
    tr = tail_ref.shape[0]
    tail_ref[...] = sh_ref[SUBLANES + tm - tr:SUBLANES + tm, :]
    _keep_tail(sh_ref, tm)


class _Group:
    def __init__(self, n_seq, seq_rows, max_tile):
        self.n_seq, self.seq_rows = n_seq, seq_rows
        self.rows = n_seq * seq_rows
        if seq_rows % max_tile == 0:
            self.tm = max_tile
        else:
            self.tm = self.rows
            assert seq_rows % SUBLANES == 0 and seq_rows >= CONV_TAPS - 1
        self.short = seq_rows < self.tm
        self.tiles_per_seq = max(seq_rows // self.tm, 1)
        self.n_tiles = self.rows // self.tm
        self.tail_rows = self.tm if self.short else SUBLANES

    def row_spec(self, c):
        return pl.BlockSpec((self.tm, c), lambda i: (i, 0))

    def tail_spec(self, c):
        return pl.BlockSpec((self.tail_rows, c), lambda i: (i, 0))

    def last_rows(self, tail, n):
        c = tail.shape[-1]
        if self.short:
            return tail.reshape(self.n_seq, self.seq_rows, c)[:, self.seq_rows - n:]
        t = tail.reshape(self.n_seq, self.tiles_per_seq, SUBLANES, c)
        return t[:, -1, SUBLANES - n:]

    def fix_rows(self, state, k):
        n_prev, c = state.shape[1], state.shape[2]
        pad = jnp.zeros((self.n_seq, self.seq_rows - k, c), state.dtype)
        return jnp.concatenate([state[:, n_prev - k:], pad], axis=1).reshape(self.rows, c)


def _whole(a):
    nd = a.ndim
    return pl.BlockSpec(a.shape, lambda *_: (0,) * nd, pipeline_mode=pl.Buffered(1))


def _params(*sem):
    return pltpu.CompilerParams(dimension_semantics=sem, vmem_limit_bytes=VMEM_LIMIT)


def _row2(v):
    return v.reshape(1, -1).astype(F32)


def _rwkv_proj(grp, x, shift, ng, mu, w0, a0, k_k, k_a, weights):
    d = x.shape[1]
    vecs = [_row2(ng), mu.astype(F32), _row2(w0), _row2(a0), _row2(k_k), _row2(k_a)]
    ins, specs = [x], [grp.row_spec(d)]
    if grp.short:
        ins.append(grp.fix_rows(shift[:, None, :], 1))
        specs.append(grp.row_spec(d))
    consts = vecs + list(weights)
    out = jax.ShapeDtypeStruct((grp.rows, d), F32)
    return pl.pallas_call(
        functools.partial(_rwkv_proj_kernel, tiles_per_seq=grp.tiles_per_seq,
                          seq_rows=grp.seq_rows),
        grid=(grp.n_tiles,),
        in_specs=specs + [_whole(a) for a in consts],
        out_specs=[grp.row_spec(d)] * 8,
        out_shape=[out] * 8,
        scratch_shapes=[pltpu.VMEM((grp.tm + 2 * SUBLANES, d), F32)],
        compiler_params=_params("arbitrary"),
        name="rwkv_proj",
    )(*ins, *consts)


def _wkv(n_seq, seq_rows, chunk, r, k, v, lw, kk, b, s0):
    d = r.shape[1]
    n_pairs = d // LANES
    step = WKV_STEP if seq_rows % WKV_STEP == 0 and WKV_STEP % chunk == 0 else chunk
    n_steps = seq_rows // step
    tok = pl.BlockSpec((step, d), lambda bi, c: (bi * n_steps + c, 0))
    st = pl.BlockSpec((None, d // HEAD, HEAD, HEAD), lambda bi, c: (bi, 0, 0, 0))
    return pl.pallas_call(
        functools.partial(_wkv_kernel, chunk=chunk),
        grid=(n_seq, n_steps),
        in_specs=[tok] * 6 + [st],
        out_specs=[tok, st],
        out_shape=[jax.ShapeDtypeStruct(r.shape, F32), jax.ShapeDtypeStruct(s0.shape, F32)],
        scratch_shapes=[pltpu.VMEM((n_pairs, LANES, LANES), F32)],
        compiler_params=_params("arbitrary", "arbitrary"),
        name="wkv",
    )(r, k, v, lw, kk, b, s0)


def _rwkv_out(grp, o, r, k, v, g, x, lng, lnb, rk, wo):
    d = x.shape[1]
    consts = [_row2(lng), _row2(lnb), _row2(rk), wo]
    return pl.pallas_call(
        _rwkv_out_kernel,
        grid=(grp.n_tiles,),
        in_specs=[grp.row_spec(d)] * 6 + [_whole(a) for a in consts],
        out_specs=grp.row_spec(d),
        out_shape=jax.ShapeDtypeStruct(x.shape, F32),
        compiler_params=_params("arbitrary"),
        name="rwkv_out",
    )(o, r, k, v, g, x, *consts)


def _fox_proj(grp, x, ng, wq, wk, wv, wg, wf, fb, qg, kg):
    d = x.shape[1]
    consts = [_row2(ng), wq, wk, wv, wg, wf, fb, qg, kg]
    wide = jax.ShapeDtypeStruct((grp.rows, d), F32)
    thin = jax.ShapeDtypeStruct((grp.rows, LANES), F32)
    out_specs = [grp.row_spec(d)] * 4 + [grp.row_spec(LANES)] * 2
    out_shape = [wide] * 4 + [thin] * 2
    if not grp.short:
        tps = grp.tiles_per_seq
        out_specs += [pl.BlockSpec((None, d, grp.tm), lambda i: (i // tps, 0, i % tps))] * 2
        out_shape += [jax.ShapeDtypeStruct((grp.n_seq, d, grp.seq_rows), F32)] * 2
    return pl.pallas_call(
        functools.partial(_fox_proj_kernel, tiles_per_seq=grp.tiles_per_seq),
        grid=(grp.n_tiles,),
        in_specs=[grp.row_spec(d)] + [_whole(a) for a in consts],
        out_specs=out_specs,
        out_shape=out_shape,
        scratch_shapes=[pltpu.VMEM((SUBLANES, LANES), F32)],
        compiler_params=_params("arbitrary"),
        name="fox_proj",
    )(x, *consts)


def _fox_attn(n_seq, seq_rows, tq, q, k, v, fcum):
    d = q.shape[1]
    n_pairs = d // LANES
    nq = seq_rows // tq
    h = d // HEAD
    f = fcum[:, :h].reshape(n_seq, nq, tq, n_pairs, 2).transpose(0, 3, 1, 2, 4)
    kv = pl.BlockSpec((None, nq, tq, LANES), lambda b, p, i: (b, 0, 0, p))
    qo = pl.BlockSpec((tq, LANES), lambda b, p, i: (b * nq + i, p))
    return pl.pallas_call(
        _fox_attn_kernel,
        grid=(n_seq, n_pairs, nq),
        in_specs=[qo, kv, kv,
                  pl.BlockSpec((None, None, nq, tq, 2), lambda b, p, i: (b, p, 0, 0, 0))],
        out_specs=qo,
        out_shape=jax.ShapeDtypeStruct(q.shape, F32),
        scratch_shapes=[pltpu.VMEM((2, nq, tq, LANES), BF16), pltpu.VMEM((nq, LANES, tq), BF16)],
        compiler_params=_params("arbitrary", "arbitrary", "arbitrary"),
        name="fox_attn",
    )(q, k.reshape(n_seq, nq, tq, d), v.reshape(n_seq, nq, tq, d), f)


def _fox_paged(n_seq, ts, q, k, v, lf, cache_k, cache_v, cache_lf, page_table):
    d = q.shape[1]
    n_heads = d // HEAD
    rows = n_heads * ts
    ps = cache_k.shape[1]
    n_pages = page_table.shape[1]
    ck = jnp.transpose(cache_k, (0, 2, 3, 1))
    cv = jnp.transpose(cache_v, (0, 2, 3, 1))
    clf = jnp.transpose(cache_lf.astype(F32), (0, 2, 1))

    n_sub = next(n for n in (PAGES_PER_STEP, 2, 1) if n_pages % n == 0)

    def slot(i):
        return lambda b, s, pt: pt[b * n_pages + n_pages - n_sub * jnp.maximum(s, 1) + i]

    tok = lambda c: pl.BlockSpec((ts, c), lambda b, s, pt: (b, 0))
    cache = [pl.BlockSpec((None, n_heads, HEAD, ps),
                          lambda b, s, pt, f=slot(i): (f(b, s, pt), 0, 0, 0)) for i in range(n_sub)]
    gates = [pl.BlockSpec((None, n_heads, ps),
                          lambda b, s, pt, f=slot(i): (f(b, s, pt), 0, 0)) for i in range(n_sub)]
    grid_spec = pltpu.PrefetchScalarGridSpec(
        num_scalar_prefetch=1,
        grid=(n_seq, n_pages // n_sub + 1),
        in_specs=[tok(d), tok(d), tok(d), tok(LANES)] + cache + cache + gates,
        out_specs=tok(d),
        scratch_shapes=[pltpu.VMEM((n_heads, ts, HEAD), F32), pltpu.VMEM((rows, 1), F32),
                        pltpu.VMEM((rows, 1), F32), pltpu.VMEM((rows, HEAD), F32),
                        pltpu.VMEM((n_heads, 1), F32)],
    )
    return pl.pallas_call(
        functools.partial(_fox_paged_kernel, n_sub=n_sub),
        grid_spec=grid_spec,
        out_shape=jax.ShapeDtypeStruct(q.shape, F32),
        compiler_params=_params("arbitrary", "arbitrary"),
        name="fox_paged",
    )(page_table.reshape(-1).astype(jnp.int32), q, k, v, lf,
      *([ck] * n_sub), *([cv] * n_sub), *([clf] * n_sub))


def _fox_out(grp, o, g, x, wo):
    d = x.shape[1]
    return pl.pallas_call(
        _fox_out_kernel,
        grid=(grp.n_tiles,),
        in_specs=[grp.row_spec(d)] * 3 + [_whole(wo)],
        out_specs=grp.row_spec(d),
        out_shape=jax.ShapeDtypeStruct(x.shape, F32),
        compiler_params=_params("arbitrary"),
        name="fox_out",
    )(o, g, x, wo)


def _conv_mix(grp, x, state, ng, win, cw, wo):
    d = x.shape[1]
    ins, specs = [x], [grp.row_spec(d)]
    if grp.short:
        ins += [grp.fix_rows(state, 1), grp.fix_rows(state, 2)]
        specs += [grp.row_spec(d)] * 2
    consts = [_row2(ng), win, cw.astype(F32), wo]
    return pl.pallas_call(
        functools.partial(_conv_mix_kernel, tiles_per_seq=grp.tiles_per_seq,
                          seq_rows=grp.seq_rows),
        grid=(grp.n_tiles,),
        in_specs=specs + [_whole(a) for a in consts],
        out_specs=[grp.row_spec(d), grp.tail_spec(d)],
        out_shape=[jax.ShapeDtypeStruct(x.shape, F32),
                   jax.ShapeDtypeStruct((grp.n_tiles * grp.tail_rows, d), F32)],
        scratch_shapes=[pltpu.VMEM((grp.tm + 2 * SUBLANES, d), F32)],
        compiler_params=_params("arbitrary"),
        name="conv_mix",
    )(*ins, *consts)


def _ffn(grp, x, state, ng, wup, cw, cb, wdn):
    d = x.shape[1]
    dff = cw.shape[1]
    ins, specs = [x], [grp.row_spec(d)]
    if grp.short:
        ins += [grp.fix_rows(state, 1), grp.fix_rows(state, 2)]
        specs += [grp.row_spec(dff)] * 2
    consts = [_row2(ng), wup, cw.astype(F32), _row2(cb), wdn]
    return pl.pallas_call(
        functools.partial(_ffn_kernel, tiles_per_seq=grp.tiles_per_seq, seq_rows=grp.seq_rows),
        grid=(grp.n_tiles,),
        in_specs=specs + [_whole(a) for a in consts],
        out_specs=[grp.row_spec(d), grp.tail_spec(dff)],
        out_shape=[jax.ShapeDtypeStruct(x.shape, F32),
                   jax.ShapeDtypeStruct((grp.n_tiles * grp.tail_rows, dff), F32)],
        scratch_shapes=[pltpu.VMEM((grp.tm + 2 * SUBLANES, dff), F32)],
        compiler_params=_params("arbitrary"),
        name="conv_ffn",
    )(*ins, *consts)


def kernel(x_prompt, x_sample, state_wkv, state_shift, cache_k, cache_v, cache_logf, state_conv_mix, state_conv_ffn, page_table, norm_mix_g, norm_ffn_g, a_mu, a_w_rkv, a_w0, a_w1, a_w2, a_a0, a_a1, a_a2, a_g1, a_g2, a_k_k, a_k_a, a_r_k, a_lnx_g, a_lnx_b, a_w_out, b_w_in, b_f_bias, b_q_norm_g, b_k_norm_g, b_w_out, c_w_in, c_conv_w, c_w_out, f_w_up, f_conv_w, f_conv_b, f_w_down):
    bp, tp, d = x_prompt.shape
    db, ts, _ = x_sample.shape
    depth = norm_mix_g.shape[0]
    n_heads = d // HEAD
    assert d % LANES == 0 and c_conv_w.shape[1] == CONV_TAPS and f_conv_w.shape[1] == CONV_TAPS
    gp = _Group(bp, tp, ROW_TILE)
    gs = _Group(db, ts, ROW_TILE)
    groups = (gp, gs)
    xs = [x_prompt.reshape(gp.rows, d), x_sample.reshape(gs.rows, d)]
    bf = lambda w: w.astype(BF16)
    chunk_p = WKV_CHUNK if tp % WKV_CHUNK == 0 else tp
    assert chunk_p <= WKV_CHUNK

    wkv_o, shift_o = ([], []), ([], [])
    k_o, v_o, lf_o = ([], []), ([], []), ([], [])
    cm_o, cf_o = ([], []), ([], [])
    for i in range(depth):
        kind, j = i % 3, i // 3
        if kind == 0:
            weights = [bf(a_w_rkv[j, 0]), bf(a_w_rkv[j, 1]), bf(a_w_rkv[j, 2]), bf(a_w1[j]),
                       bf(a_w2[j]), bf(a_a1[j]), bf(a_a2[j]), bf(a_g1[j]), bf(a_g2[j])]
            wo = bf(a_w_out[j])
            for gi, grp in enumerate(groups):
                shift = state_shift[j] if gi == 1 else None
                r, k, v, lw, kk, b, g, xn = _rwkv_proj(
                    grp, xs[gi], shift, norm_mix_g[i], a_mu[j], a_w0[j], a_a0[j], a_k_k[j],
                    a_k_a[j], weights)
                if gi == 0:
                    s0 = jnp.zeros((grp.n_seq, n_heads, HEAD, HEAD), F32)
                    chunk = chunk_p
                else:
                    s0 = state_wkv[j].astype(F32)
                    chunk = ts
                o, s_new = _wkv(grp.n_seq, grp.seq_rows, chunk, r, k, v, lw, kk, b, s0)
                xs[gi] = _rwkv_out(grp, o, r, k, v, g, xs[gi], a_lnx_g[j], a_lnx_b[j],
                                   a_r_k[j].reshape(-1), wo)
                wkv_o[gi].append(s_new.astype(state_wkv.dtype if gi else F32))
                shift_o[gi].append(xn.reshape(grp.n_seq, grp.seq_rows, d)[:, -1])
        elif kind == 1:
            w_in = b_w_in[j]
            wq, wk, wv, wg = (bf(w_in[:, c * d:(c + 1) * d]) for c in range(4))
            wf = bf(jnp.pad(w_in[:, 4 * d:], ((0, 0), (0, LANES - n_heads))))
            fb = jnp.pad(b_f_bias[j].astype(F32), (0, LANES - n_heads)).reshape(1, LANES)
            qg = jnp.tile(b_q_norm_g[j].astype(F32), n_heads).reshape(1, d)
            kg = jnp.tile(b_k_norm_g[j].astype(F32), n_heads).reshape(1, d)
            wo = bf(b_w_out[j])
            for gi, grp in enumerate(groups):
                q, k, v, g, lf, fc, *kv_t = _fox_proj(grp, xs[gi], norm_mix_g[i], wq, wk, wv, wg,
                                                      wf, fb, qg, kg)
                if gi == 0:
                    tile = ATTN_TILE if grp.seq_rows % ATTN_TILE == 0 else grp.tm
                    o = _fox_attn(grp.n_seq, grp.seq_rows, tile, q, k, v, fc)
                else:
                    o = _fox_paged(grp.n_seq, grp.seq_rows, q, k, v, lf, cache_k[j], cache_v[j],
                                   cache_logf[j], page_table)
                xs[gi] = _fox_out(grp, o, g, xs[gi], wo)
                shape = (grp.n_seq, grp.seq_rows, n_heads, HEAD)
                if kv_t:
                    rows_last = lambda t: jnp.transpose(
                        t.reshape(grp.n_seq, n_heads, HEAD, grp.seq_rows), (0, 3, 1, 2))
                    k_o[gi].append(rows_last(kv_t[0]))
                    v_o[gi].append(rows_last(kv_t[1]))
                else:
                    k_o[gi].append(k.reshape(shape))
                    v_o[gi].append(v.reshape(shape))
                lf_o[gi].append(lf[:, :n_heads].reshape(shape[:3]))
        else:
            win, wo = bf(c_w_in[j]), bf(c_w_out[j])
            for gi, grp in enumerate(groups):
                state = state_conv_mix[j] if gi == 1 else None
                xs[gi], tail = _conv_mix(grp, xs[gi], state, norm_mix_g[i], win, c_conv_w[j], wo)
                cm_o[gi].append(grp.last_rows(tail, CONV_TAPS - 1))
        wup, wdn = bf(f_w_up[i]), bf(f_w_down[i])
        for gi, grp in enumerate(groups):
            state = state_conv_ffn[i] if gi == 1 else None
            xs[gi], tail = _ffn(grp, xs[gi], state, norm_ffn_g[i], wup, f_conv_w[i],
                                f_conv_b[i], wdn)
            cf_o[gi].append(grp.last_rows(tail, CONV_TAPS - 1))

    st = jnp.stack
    return (xs[0].reshape(bp, tp, d), xs[1].reshape(db, ts, d),
            st(wkv_o[0]), st(shift_o[0]), st(wkv_o[1]), st(shift_o[1]),
            st(k_o[0]), st(v_o[0]), st(lf_o[0]), st(k_o[1]), st(v_o[1]), st(lf_o[1]),
            st(cm_o[0]), st(cm_o[1]), st(cf_o[0]), st(cf_o[1]))
```
